```python
import jax, jax.numpy as jnp
from jax import lax
import numpy as np

D_MODEL = 1024
BATCH = 1
SEQ = 16384
DEPTH = 2
DEC_BATCH = 128
DEC_SEQ = 4
PAST_LEN = 16384
PAGE_SIZE = 128

N_MIXERS = 2
MOBA_HEADS = 16
MOBA_KV_HEADS = 4
MOBA_HEAD_DIM = 64
MOBA_GROUP = MOBA_HEADS // MOBA_KV_HEADS
MOBA_BLOCK = 256
MOBA_TOPK = 3
MOBA_Q_CHUNK = 64
MLA_HEADS = 16
MLA_Q_LORA = 256
MLA_KV_LORA = 128
MLA_NOPE = 64
MLA_ROPE = 32
MLA_V = 64
MLA_Q_CHUNK = 128
ROPE_THETA = 10000.0
N_GROUPS = 4
EXPERTS_PER_GROUP = 8
N_EXPERTS = N_GROUPS * EXPERTS_PER_GROUP
EXPERT_TOPK = 2
D_EXPERT = 256
MOE_CHUNK = 128
NORM_EPS = 1e-6
N_ADA = 6

kernel_name = 'hybrid_moba_mla_hmoe_adaln_step'

F32 = jnp.float32


def rmsnorm(x, g):
    xf = x.astype(F32)
    y = xf * lax.rsqrt(jnp.mean(xf * xf, axis=-1, keepdims=True) + NORM_EPS)
    return (y * g.astype(F32)).astype(x.dtype)


def modulate(x, g, shift, scale):
    return rmsnorm(x, g) * (1 + scale) + shift


def ada_mod(c, w, b):
    m = (jax.nn.silu(c) @ w + b).reshape(c.shape[0], N_ADA, D_MODEL)
    return tuple(m[:, i, None, :] for i in range(N_ADA))


def alibi_slopes(n_heads):
    return jnp.exp2(-8.0 * jnp.arange(1, n_heads + 1, dtype=F32) / n_heads)


def rope_tables(pos):
    inv = ROPE_THETA ** (-jnp.arange(0, MLA_ROPE, 2, dtype=F32) / MLA_ROPE)
    ang = pos.astype(F32)[:, None] * inv[None, :]
    return jnp.cos(ang), jnp.sin(ang)


def apply_rope(x, cos, sin):
    half = x.shape[-1] // 2
    x1, x2 = x[..., :half], x[..., half:]
    return jnp.concatenate([x1 * cos - x2 * sin, x2 * cos + x1 * sin], axis=-1).astype(x.dtype)


def moba_qkv(h, w_qkv):
    B, S, _ = h.shape
    nq = MOBA_HEADS * MOBA_HEAD_DIM
    nk = MOBA_KV_HEADS * MOBA_HEAD_DIM
    qkv = h @ w_qkv
    q = qkv[..., :nq].reshape(B, S, MOBA_KV_HEADS, MOBA_GROUP, MOBA_HEAD_DIM)
    k = qkv[..., nq:nq + nk].reshape(B, S, MOBA_KV_HEADS, MOBA_HEAD_DIM)
    v = qkv[..., nq + nk:].reshape(B, S, MOBA_KV_HEADS, MOBA_HEAD_DIM)
    return q, k, v


def moba_prompt(h, w_qkv, w_o):
    B, S, _ = h.shape
    q, k, v = moba_qkv(h, w_qkv)
    nb = -(-S // MOBA_BLOCK)
    pad = nb * MOBA_BLOCK - S
    kp = jnp.pad(k, ((0, 0), (0, pad), (0, 0), (0, 0)))
    vp = jnp.pad(v, ((0, 0), (0, pad), (0, 0), (0, 0)))
    k_bt = kp.reshape(B, nb, MOBA_BLOCK, MOBA_KV_HEADS, MOBA_HEAD_DIM).transpose(0, 3, 1, 2, 4)
    v_bt = vp.reshape(B, nb, MOBA_BLOCK, MOBA_KV_HEADS, MOBA_HEAD_DIM).transpose(0, 3, 1, 2, 4)
    k_mean = jnp.mean(k_bt.astype(F32), axis=3)
    n_sel = min(MOBA_TOPK, nb)
    slopes = alibi_slopes(MOBA_HEADS).reshape(MOBA_KV_HEADS, MOBA_GROUP)
    scale = MOBA_HEAD_DIM ** -0.5
    C = MOBA_Q_CHUNK
    n_chunks = S // C
    q_chunks = q.reshape(B, n_chunks, C, MOBA_KV_HEADS, MOBA_GROUP, MOBA_HEAD_DIM).transpose(1, 0, 2, 3, 4, 5)
    b_idx = jnp.arange(B)[:, None, None, None, None]
    kvh_idx = jnp.arange(MOBA_KV_HEADS)[None, None, :, None, None]
    blk_ids = jnp.arange(nb)
    sel_ids = jnp.arange(n_sel)
    in_blk = jnp.arange(MOBA_BLOCK)

    def one_chunk(args):
        ci, qc = args
        t = ci * C + jnp.arange(C)
        own = (ci * C) // MOBA_BLOCK
        gs = jnp.einsum('bckgd,bknd->bckgn', qc, k_mean, preferred_element_type=F32)
        gs = jnp.where(blk_ids < own, gs, -jnp.inf)
        _, top_i = lax.top_k(gs, n_sel)
        k_sel = k_bt[b_idx, kvh_idx, top_i]
        v_sel = v_bt[b_idx, kvh_idx, top_i]
        s_pos = top_i[..., None] * MOBA_BLOCK + in_blk
        dist_sel = (t[None, :, None, None, None, None] - s_pos).astype(F32)
        s_sel = (jnp.einsum('bckgd,bckgjld->bckgjl', qc, k_sel, preferred_element_type=F32) * scale
                 - slopes[:, :, None, None] * dist_sel)
        s_sel = jnp.where((sel_ids < own)[:, None], s_sel, -jnp.inf)
        k_own = lax.dynamic_index_in_dim(k_bt, own, axis=2, keepdims=False)
        v_own = lax.dynamic_index_in_dim(v_bt, own, axis=2, keepdims=False)
        dist_own = (t[:, None] - (own * MOBA_BLOCK + in_blk)[None, :]).astype(F32)
        s_own = (jnp.einsum('bckgd,bkld->bckgl', qc, k_own, preferred_element_type=F32) * scale
                 - slopes[:, :, None] * dist_own[:, None, None, :])
        s_own = jnp.where(dist_own[:, None, None, :] >= 0, s_own, -jnp.inf)
        s = jnp.concatenate([s_sel.reshape(B, C, MOBA_KV_HEADS, MOBA_GROUP, n_sel * MOBA_BLOCK), s_own], axis=-1)
        p = jax.nn.softmax(s, axis=-1).astype(v.dtype)
        p_sel = p[..., :n_sel * MOBA_BLOCK].reshape(B, C, MOBA_KV_HEADS, MOBA_GROUP, n_sel, MOBA_BLOCK)
        p_own = p[..., n_sel * MOBA_BLOCK:]
        o = (jnp.einsum('bckgjl,bckgjld->bckgd', p_sel, v_sel, preferred_element_type=F32)
             + jnp.einsum('bckgl,bkld->bckgd', p_own, v_own, preferred_element_type=F32))
        return o.astype(h.dtype)

    o = lax.map(one_chunk, (jnp.arange(n_chunks), q_chunks))
    o = o.transpose(1, 0, 2, 3, 4, 5).reshape(B, S, MOBA_HEADS * MOBA_HEAD_DIM)
    return o @ w_o, k, v


def moba_sample(h, cache_k, cache_v, page_table, w_qkv, w_o):
    Bd, T, _ = h.shape
    q, k, v = moba_qkv(h, w_qkv)
    page = cache_k.shape[1]
    past = page_table.shape[1] * page
    n_full = past // MOBA_BLOCK
    own_start = n_full * MOBA_BLOCK
    n_sel = min(MOBA_TOPK, n_full)
    k_rows = cache_k.reshape(-1, MOBA_KV_HEADS, MOBA_HEAD_DIM)
    v_rows = cache_v.reshape(-1, MOBA_KV_HEADS, MOBA_HEAD_DIM)
    slopes = alibi_slopes(MOBA_HEADS).reshape(MOBA_KV_HEADS, MOBA_GROUP)
    scale = MOBA_HEAD_DIM ** -0.5
    t = past + jnp.arange(T)
    loc_pos = jnp.arange(own_start, past + T)
    dist_loc = (t[:, None] - loc_pos[None, :]).astype(F32)
    kvh_idx = jnp.arange(MOBA_KV_HEADS)[None, :, None, None]
    in_blk = jnp.arange(MOBA_BLOCK)

    def one_seq(args):
        table, qs, ks, vs = args
        rows = (table[:, None] * page + jnp.arange(page)[None, :]).reshape(past)
        k_loc = jnp.concatenate([k_rows[rows[own_start:]], ks], axis=0)
        v_loc = jnp.concatenate([v_rows[rows[own_start:]], vs], axis=0)
        s_loc = (jnp.einsum('tkgd,lkd->tkgl', qs, k_loc, preferred_element_type=F32) * scale
                 - slopes[None, :, :, None] * dist_loc[:, None, None, :])
        s_loc = jnp.where(dist_loc[:, None, None, :] >= 0, s_loc, -jnp.inf)
        if n_sel > 0:
            k_past = k_rows[rows[:own_start]]
            kb = k_past.reshape(n_full, MOBA_BLOCK, MOBA_KV_HEADS, MOBA_HEAD_DIM).transpose(2, 0, 1, 3)
            k_mean = jnp.mean(kb.astype(F32), axis=2)
            gs = jnp.einsum('tkgd,knd->tkgn', qs, k_mean, preferred_element_type=F32)
            _, top_i = lax.top_k(gs, n_sel)
            k_sel = kb[kvh_idx, top_i]
            s_pos = top_i[..., None] * MOBA_BLOCK + in_blk
            v_sel = v_rows[rows[s_pos], kvh_idx[..., None]]
            dist_sel = (t[:, None, None, None, None] - s_pos).astype(F32)
            s_sel = (jnp.einsum('tkgd,tkgjld->tkgjl', qs, k_sel, preferred_element_type=F32) * scale
                     - slopes[None, :, :, None, None] * dist_sel)
            s = jnp.concatenate([s_sel.reshape(T, MOBA_KV_HEADS, MOBA_GROUP, n_sel * MOBA_BLOCK), s_loc], axis=-1)
            p = jax.nn.softmax(s, axis=-1).astype(vs.dtype)
            p_sel = p[..., :n_sel * MOBA_BLOCK].reshape(T, MOBA_KV_HEADS, MOBA_GROUP, n_sel, MOBA_BLOCK)
            o = (jnp.einsum('tkgjl,tkgjld->tkgd', p_sel, v_sel, preferred_element_type=F32)
                 + jnp.einsum('tkgl,lkd->tkgd', p[..., n_sel * MOBA_BLOCK:], v_loc, preferred_element_type=F32))
        else:
            p = jax.nn.softmax(s_loc, axis=-1).astype(vs.dtype)
            o = jnp.einsum('tkgl,lkd->tkgd', p, v_loc, preferred_element_type=F32)
        return o.astype(h.dtype)

    o = lax.map(one_seq, (page_table, q, k, v))
    o = o.reshape(Bd, T, MOBA_HEADS * MOBA_HEAD_DIM)
    return o @ w_o, k, v


def mla_qkv(h, pos, w_dq, g_q, w_uq, w_dkv, g_kv):
    B, S, _ = h.shape
    cq = rmsnorm(h @ w_dq, g_q)
    q = (cq @ w_uq).reshape(B, S, MLA_HEADS, MLA_NOPE + MLA_ROPE)
    kv = h @ w_dkv
    ckv = rmsnorm(kv[..., :MLA_KV_LORA], g_kv)
    cos, sin = rope_tables(pos)
    q_rope = apply_rope(q[..., MLA_NOPE:], cos[:, None, :], sin[:, None, :])
    k_rope = apply_rope(kv[..., MLA_KV_LORA:], cos, sin)
    return q[..., :MLA_NOPE], q_rope, ckv, k_rope


def mla_prompt(h, w_dq, g_q, w_uq, w_dkv, g_kv, w_uk, w_uv, w_o):
    B, S, _ = h.shape
    pos = jnp.arange(S)
    q_nope, q_rope, ckv, k_rope = mla_qkv(h, pos, w_dq, g_q, w_uq, w_dkv, g_kv)
    k_nope = jnp.einsum('bsc,chd->bshd', ckv, w_uk)
    v = jnp.einsum('bsc,chd->bshd', ckv, w_uv)
    scale = (MLA_NOPE + MLA_ROPE) ** -0.5
    C = MLA_Q_CHUNK
    n_chunks = S // C
    qn_c = q_nope.reshape(B, n_chunks, C, MLA_HEADS, MLA_NOPE).transpose(1, 0, 2, 3, 4)
    qr_c = q_rope.reshape(B, n_chunks, C, MLA_HEADS, MLA_ROPE).transpose(1, 0, 2, 3, 4)

    def one_chunk(args):
        ci, qn, qr = args
        t = ci * C + jnp.arange(C)
        s = (jnp.einsum('bqhd,bshd->bhqs', qn, k_nope, preferred_element_type=F32)
             + jnp.einsum('bqhr,bsr->bhqs', qr, k_rope, preferred_element_type=F32)) * scale
        s = jnp.where(pos[None, :] <= t[:, None], s, -jnp.inf)
        p = jax.nn.softmax(s, axis=-1).astype(v.dtype)
        return jnp.einsum('bhqs,bshd->bqhd', p, v, preferred_element_type=F32).astype(h.dtype)

    o = lax.map(one_chunk, (jnp.arange(n_chunks), qn_c, qr_c))
    o = o.transpose(1, 0, 2, 3, 4).reshape(B, S, MLA_HEADS * MLA_V)
    return o @ w_o, ckv, k_rope


def mla_sample(h, cache_ckv, cache_krope, page_table, w_dq, g_q, w_uq, w_dkv, g_kv, w_uk, w_uv, w_o):
    Bd, T, _ = h.shape
    page = cache_ckv.shape[1]
    past = page_table.shape[1] * page
    pos = past + jnp.arange(T)
    q_nope, q_rope, ckv, k_rope = mla_qkv(h, pos, w_dq, g_q, w_uq, w_dkv, g_kv)
    q_lat = jnp.einsum('bthd,chd->bthc', q_nope, w_uk)
    scale = (MLA_NOPE + MLA_ROPE) ** -0.5
    c_rows = cache_ckv.reshape(-1, MLA_KV_LORA)
    r_rows = cache_krope.reshape(-1, MLA_ROPE)
    mask = jnp.concatenate([jnp.ones((T, past), dtype=bool),
                            jnp.arange(T)[:, None] >= jnp.arange(T)[None, :]], axis=1)

    def one_seq(args):
        table, ql, qr, cn, kn = args
        rows = (table[:, None] * page + jnp.arange(page)[None, :]).reshape(past)
        c_all = jnp.concatenate([c_rows[rows], cn], axis=0)
        r_all = jnp.concatenate([r_rows[rows], kn], axis=0)
        s = (jnp.einsum('thc,sc->ths', ql, c_all, preferred_element_type=F32)
             + jnp.einsum('thr,sr->ths', qr, r_all, preferred_element_type=F32)) * scale
        s = jnp.where(mask[:, None, :], s, -jnp.inf)
        p = jax.nn.softmax(s, axis=-1).astype(c_all.dtype)
        return jnp.einsum('ths,sc->thc', p, c_all, preferred_element_type=F32).astype(h.dtype)

    o_lat = lax.map(one_seq, (page_table, q_lat, q_rope, ckv, k_rope))
    o = jnp.einsum('bthc,chd->bthd', o_lat, w_uv).reshape(Bd, T, MLA_HEADS * MLA_V)
    return o @ w_o, ckv, k_rope


def hier_moe(h, w_group, w_expert, w_gu, w_down):
    T = h.shape[0]
    tok = jnp.arange(T)
    g_logits = jnp.einsum('td,dg->tg', h, w_group, preferred_element_type=F32)
    g_prob = jax.nn.softmax(g_logits, axis=-1)
    g_sel = jnp.argmax(g_logits, axis=-1)
    g_p = g_prob[tok, g_sel][:, None]
    e_logits = jnp.einsum('td,de->te', h, w_expert, preferred_element_type=F32).reshape(T, N_GROUPS, EXPERTS_PER_GROUP)
    e_prob = jax.nn.softmax(e_logits[tok, g_sel], axis=-1)
    top_p, top_i = lax.top_k(e_prob, EXPERT_TOPK)
    wts = top_p / jnp.sum(top_p, axis=-1, keepdims=True) * g_p
    eid = g_sel[:, None] * EXPERTS_PER_GROUP + top_i
    gate = jnp.einsum('tk,tke->te', wts, jax.nn.one_hot(eid, N_EXPERTS, dtype=F32))
    gu = jnp.einsum('td,edf->tef', h, w_gu)
    a = jax.nn.silu(gu[..., :D_EXPERT]) * gu[..., D_EXPERT:]
    return jnp.einsum('tef,efd->td', a * gate[..., None].astype(a.dtype), w_down)


def moe_prompt(h, w_group, w_expert, w_gu, w_down):
    B, S, D = h.shape
    n = S // MOE_CHUNK
    hc = h.reshape(B, n, MOE_CHUNK, D).transpose(1, 0, 2, 3).reshape(n, B * MOE_CHUNK, D)
    y = lax.map(lambda t: hier_moe(t, w_group, w_expert, w_gu, w_down), hc)
    return y.reshape(n, B, MOE_CHUNK, D).transpose(1, 0, 2, 3).reshape(B, S, D)


def setup_inputs(seed: int = 0) -> dict:
    key = jax.random.key(seed)
    ks = jax.random.split(key, 32)
    d = D_MODEL
    n_pages = PAST_LEN // PAGE_SIZE
    n_used = DEC_BATCH * n_pages
    n_pool = n_used + n_used // 4

    def nrm(k, shape, scale=1.0):
        return jax.random.normal(k, shape, F32) * scale

    def gain(k, shape):
        return 1.0 + 0.02 * jax.random.normal(k, shape, F32)

    page_table = jax.random.permutation(ks[0], n_pool)[:n_used].reshape(DEC_BATCH, n_pages).astype(jnp.int32)
    qkv_w = MOBA_HEADS * MOBA_HEAD_DIM + 2 * MOBA_KV_HEADS * MOBA_HEAD_DIM
    return {
        'x_prompt': nrm(ks[1], (BATCH, SEQ, d)),
        'x_sample': nrm(ks[2], (DEC_BATCH, DEC_SEQ, d)),
        'cache_moba_k': nrm(ks[3], (n_pool, PAGE_SIZE, MOBA_KV_HEADS, MOBA_HEAD_DIM)),
        'cache_moba_v': nrm(ks[4], (n_pool, PAGE_SIZE, MOBA_KV_HEADS, MOBA_HEAD_DIM)),
        'cache_mla_ckv': nrm(ks[5], (n_pool, PAGE_SIZE, MLA_KV_LORA)),
        'cache_mla_krope': nrm(ks[6], (n_pool, PAGE_SIZE, MLA_ROPE)),
        'page_table': page_table,
        'c_prompt': nrm(ks[7], (BATCH, d)),
        'c_sample': nrm(ks[8], (DEC_BATCH, d)),
        'w_ada': nrm(ks[9], (DEPTH, d, N_ADA * d), 0.5 * d ** -0.5),
        'b_ada': nrm(ks[10], (DEPTH, N_ADA * d), 0.02),
        'g_norm1': gain(ks[11], (DEPTH, d)),
        'g_norm2': gain(ks[12], (DEPTH, d)),
        'moba_w_qkv': nrm(ks[13], (d, qkv_w), d ** -0.5),
        'moba_w_o': nrm(ks[14], (MOBA_HEADS * MOBA_HEAD_DIM, d), (MOBA_HEADS * MOBA_HEAD_DIM) ** -0.5),
        'mla_w_dq': nrm(ks[15], (d, MLA_Q_LORA), d ** -0.5),
        'mla_g_q': gain(ks[16], (MLA_Q_LORA,)),
        'mla_w_uq': nrm(ks[17], (MLA_Q_LORA, MLA_HEADS * (MLA_NOPE + MLA_ROPE)), MLA_Q_LORA ** -0.5),
        'mla_w_dkv': nrm(ks[18], (d, MLA_KV_LORA + MLA_ROPE), d ** -0.5),
        'mla_g_kv': gain(ks[19], (MLA_KV_LORA,)),
        'mla_w_uk': nrm(ks[20], (MLA_KV_LORA, MLA_HEADS, MLA_NOPE), MLA_KV_LORA ** -0.5),
        'mla_w_uv': nrm(ks[21], (MLA_KV_LORA, MLA_HEADS, MLA_V), MLA_KV_LORA ** -0.5),
        'mla_w_o': nrm(ks[22], (MLA_HEADS * MLA_V, d), (MLA_HEADS * MLA_V) ** -0.5),
        'moe_w_group': nrm(ks[23], (DEPTH, d, N_GROUPS), d ** -0.5),
        'moe_w_expert': nrm(ks[24], (DEPTH, d, N_EXPERTS), d ** -0.5),
        'moe_w_gu': nrm(ks[25], (DEPTH, N_EXPERTS, d, 2 * D_EXPERT), d ** -0.5),
        'moe_w_down': nrm(ks[26], (DEPTH, N_EXPERTS, D_EXPERT, d), D_EXPERT ** -0.5),
        'g_final': gain(ks[27], (d,)),
    }


def reference(x_prompt, x_sample, cache_moba_k, cache_moba_v, cache_mla_ckv, cache_mla_krope, page_table,
              c_prompt, c_sample, w_ada, b_ada, g_norm1, g_norm2, moba_w_qkv, moba_w_o,
              mla_w_dq, mla_g_q, mla_w_uq, mla_w_dkv, mla_g_kv, mla_w_uk, mla_w_uv, mla_w_o,
              moe_w_group, moe_w_expert, moe_w_gu, moe_w_down, g_final):
    xp, xs = x_prompt, x_sample
    for layer in range(DEPTH):
        sp1, cp1, gp1, sp2, cp2, gp2 = ada_mod(c_prompt, w_ada[layer], b_ada[layer])
        ss1, cs1, gs1, ss2, cs2, gs2 = ada_mod(c_sample, w_ada[layer], b_ada[layer])
        hp = modulate(xp, g_norm1[layer], sp1, cp1)
        hs = modulate(xs, g_norm1[layer], ss1, cs1)
        if layer % N_MIXERS == 0:
            op, moba_k_prompt, moba_v_prompt = moba_prompt(hp, moba_w_qkv, moba_w_o)
            osm, moba_k_sample, moba_v_sample = moba_sample(hs, cache_moba_k, cache_moba_v, page_table,
                                                            moba_w_qkv, moba_w_o)
        else:
            op, mla_ckv_prompt, mla_krope_prompt = mla_prompt(hp, mla_w_dq, mla_g_q, mla_w_uq, mla_w_dkv,
                                                              mla_g_kv, mla_w_uk, mla_w_uv, mla_w_o)
            osm, mla_ckv_sample, mla_krope_sample = mla_sample(hs, cache_mla_ckv, cache_mla_krope, page_table,
                                                               mla_w_dq, mla_g_q, mla_w_uq, mla_w_dkv,
                                                               mla_g_kv, mla_w_uk, mla_w_uv, mla_w_o)
        xp = xp + gp1 * op
        xs = xs + gs1 * osm
        hp = modulate(xp, g_norm2[layer], sp2, cp2)
        hs = modulate(xs, g_norm2[layer], ss2, cs2)
        xp = xp + gp2 * moe_prompt(hp, moe_w_group[layer], moe_w_expert[layer], moe_w_gu[layer], moe_w_down[layer])
        xs = xs + gs2 * hier_moe(hs.reshape(-1, D_MODEL), moe_w_group[layer], moe_w_expert[layer],
                                 moe_w_gu[layer], moe_w_down[layer]).reshape(xs.shape)
    y_prompt = rmsnorm(xp, g_final)
    y_sample = rmsnorm(xs, g_final)
    return (y_prompt, y_sample, moba_k_prompt, moba_v_prompt, mla_ckv_prompt, mla_krope_prompt,
            moba_k_sample, moba_v_sample, mla_ckv_sample, mla_krope_sample)
```

```python
import functools

import jax
import jax.numpy as jnp
from jax import lax
from jax.experimental import pallas as pl
from jax.experimental.pallas import tpu as pltpu

F32 = jnp.float32
BF16 = jnp.bfloat16
HIGHEST = lax.Precision.HIGHEST

D_MODEL = 1024
N_ADA = 6
NORM_EPS = 1e-6
MOBA_HEADS = 16
MOBA_KV_HEADS = 4
MOBA_GROUP = MOBA_HEADS // MOBA_KV_HEADS
MOBA_HEAD_DIM = 64
MOBA_BLOCK = 256
MOBA_TOPK = 3
MLA_HEADS = 16
MLA_Q_LORA = 256
MLA_KV_LORA = 128
MLA_NOPE = 64
MLA_ROPE = 32
MLA_V = 64
ROPE_THETA = 10000.0
N_GROUPS = 4
EXPERTS_PER_GROUP = 8
N_EXPERTS = N_GROUPS * EXPERTS_PER_GROUP
D_EXPERT = 256

LANES = 128
NEG = -1e30
VMEM_LIMIT = 56 * 1024 * 1024
PAGES_PER_STEP = 16

_NT = (((1,), (1,)), ((), ()))


def _nt_dot(a, b, **kw):
    return lax.dot_general(a, b, _NT, preferred_element_type=F32, **kw)


def _cparams(*sem):
    return pltpu.CompilerParams(dimension_semantics=sem, vmem_limit_bytes=VMEM_LIMIT)


def _rms_mod(x, g, shift, scale):
    y = x * lax.rsqrt(jnp.mean(x * x, axis=-1, keepdims=True) + NORM_EPS)
    return (y * g) * (1.0 + scale) + shift


def _top_mask_bias(vals, lanef, n_top, bias):
    for _ in range(n_top):
        m = jnp.max(vals, axis=-1, keepdims=True)
        idx = jnp.min(jnp.where(vals == m, lanef, 1e9), axis=-1, keepdims=True)
        idx = jnp.where(m > -jnp.inf, idx, -1.0)
        pick = lanef == idx
        bias = jnp.where(pick, 0.0, bias)
        vals = jnp.where(pick, -jnp.inf, vals)
    return bias


def _ada_kernel(c_ref, w_ref, b_ref, o_ref):
    c = c_ref[...]
    a = c * jax.nn.sigmoid(c)
    o_ref[0] = jnp.dot(a, w_ref[0], preferred_element_type=F32, precision=HIGHEST) + b_ref[0]


def _ada_call(c_all, w_ada, b_ada):
    depth, d, n = w_ada.shape
    rows = c_all.shape[0]
    tn = 512
    return pl.pallas_call(
        _ada_kernel,
        grid=(depth, n // tn),
        in_specs=[
            pl.BlockSpec((rows, d), lambda l, j: (0, 0)),
            pl.BlockSpec((1, d, tn), lambda l, j: (l, 0, j)),
            pl.BlockSpec((1, 1, tn), lambda l, j: (l, 0, j)),
        ],
        out_specs=pl.BlockSpec((1, rows, tn), lambda l, j: (l, 0, j)),
        out_shape=jax.ShapeDtypeStruct((depth, rows, n), F32),
        compiler_params=_cparams("arbitrary", "arbitrary"),
        name="ada_mod",
    )(c_all, w_ada, b_ada.reshape(depth, 1, n))


def _mod_spec(mod, tm):
    d = mod.shape[1]
    if mod.shape[0] == 1:
        return pl.BlockSpec((1, d), lambda i, *_: (0, 0))
    return pl.BlockSpec((tm, d), lambda i, *_: (i, 0))


def _norm_matmul_kernel(x_ref, g_ref, sh_ref, sc_ref, w_ref, o_ref):
    h = _rms_mod(x_ref[...], g_ref[...], sh_ref[...], sc_ref[...]).astype(BF16)
    o_ref[...] = jnp.dot(h, w_ref[...], preferred_element_type=F32)


def _norm_matmul(x, g, shift, scale, w, tm):
    m, d = x.shape
    tm = min(tm, m)
    n = w.shape[1]
    return pl.pallas_call(
        _norm_matmul_kernel,
        grid=(m // tm,),
        in_specs=[
            pl.BlockSpec((tm, d), lambda i: (i, 0)),
            pl.BlockSpec((1, d), lambda i: (0, 0)),
            _mod_spec(shift, tm),
            _mod_spec(scale, tm),
            pl.BlockSpec((d, n), lambda i: (0, 0)),
        ],
        out_specs=pl.BlockSpec((tm, n), lambda i: (i, 0)),
        out_shape=jax.ShapeDtypeStruct((m, n), F32),
        compiler_params=_cparams("arbitrary"),
        name="norm_matmul",
    )(x, g, shift, scale, w)


def _mm_res_kernel(a_ref, w_ref, x_ref, gate_ref, o_ref):
    y = jnp.dot(a_ref[...], w_ref[...], preferred_element_type=F32)
    o_ref[...] = x_ref[...] + gate_ref[...] * y


def _mm_res(a, w, x, gate, tm):
    m, k = a.shape
    tm = min(tm, m)
    n = w.shape[1]
    return pl.pallas_call(
        _mm_res_kernel,
        grid=(m // tm,),
        in_specs=[
            pl.BlockSpec((tm, k), lambda i: (i, 0)),
            pl.BlockSpec((k, n), lambda i: (0, 0)),
            pl.BlockSpec((tm, n), lambda i: (i, 0)),
            _mod_spec(gate, tm),
        ],
        out_specs=pl.BlockSpec((tm, n), lambda i: (i, 0)),
        out_shape=jax.ShapeDtypeStruct((m, n), F32),
        compiler_params=_cparams("arbitrary"),
        name="matmul_residual",
    )(a, w, x, gate)


def _bmm_kernel(a_ref, b_ref, o_ref):
    o_ref[0] = jnp.dot(a_ref[0], b_ref[0], preferred_element_type=F32, precision=HIGHEST)


def _bmm(a, b):
    n, m, k = a.shape
    p = b.shape[2]
    return pl.pallas_call(
        _bmm_kernel,
        grid=(n,),
        in_specs=[pl.BlockSpec((1, m, k), lambda i: (i, 0, 0)), pl.BlockSpec((1, k, p), lambda i: (i, 0, 0))],
        out_specs=pl.BlockSpec((1, m, p), lambda i: (i, 0, 0)),
        out_shape=jax.ShapeDtypeStruct((n, m, p), F32),
        compiler_params=_cparams("arbitrary"),
        name="fold_weights",
    )(a, b)


_QW = MOBA_HEADS * LANES
_KW = MOBA_KV_HEADS * LANES


def _moba_qkv_kernel(x_ref, g_ref, sh_ref, sc_ref, w_ref, q_ref, ka_ref, vp_ref, ko_ref, vo_ref, km_ref):
    i = pl.program_id(0)
    tm = x_ref.shape[0]
    half = LANES // 2

    @pl.when(i == 0)
    def _():
        km_ref[...] = jnp.zeros_like(km_ref)

    h = _rms_mod(x_ref[...], g_ref[...], sh_ref[...], sc_ref[...]).astype(BF16)
    y = jnp.dot(h, w_ref[...], preferred_element_type=F32)
    lane = lax.broadcasted_iota(jnp.int32, (tm, LANES), 1)
    lanef = lane.astype(F32)
    kvw = MOBA_KV_HEADS * MOBA_HEAD_DIM
    ko_ref[...] = y[:, _QW + 2 * _KW:_QW + 2 * _KW + kvw]
    vo_ref[...] = y[:, _QW + 2 * _KW + kvw:_QW + 2 * _KW + 2 * kvw]
    for k in range(MOBA_KV_HEADS):
        kp = y[:, _QW + k * LANES:_QW + (k + 1) * LANES]
        km_ref[k, pl.ds(half + i, 1), :] = jnp.mean(kp, axis=0, keepdims=True)
        ka_ref[:, k * LANES:(k + 1) * LANES] = jnp.where(lane == half + i, 1.0, kp).astype(BF16)
        vp_ref[:, k * LANES:(k + 1) * LANES] = y[:, _QW + _KW + k * LANES:_QW + _KW + (k + 1) * LANES].astype(BF16)
    past = (lane >= half) & (lane < half + i)
    own_bias = jnp.where(lane == half + i, 0.0, NEG)
    for hd in range(MOBA_HEADS):
        qp = y[:, hd * LANES:(hd + 1) * LANES]
        gs = _nt_dot(qp, km_ref[hd // MOBA_GROUP], precision=HIGHEST)
        bias = _top_mask_bias(jnp.where(past, gs, -jnp.inf), lanef, MOBA_TOPK, own_bias)
        q_ref[:, hd * LANES:(hd + 1) * LANES] = jnp.where(lane < half, qp * MOBA_HEAD_DIM ** -0.5, bias).astype(BF16)


def _moba_qkv_call(x, g, shift, scale, w):
    t, d = x.shape
    tm = MOBA_BLOCK
    assert t // tm <= LANES // 2
    n = w.shape[1]
    kvw = MOBA_KV_HEADS * MOBA_HEAD_DIM
    return pl.pallas_call(
        _moba_qkv_kernel,
        grid=(t // tm,),
        in_specs=[
            pl.BlockSpec((tm, d), lambda i: (i, 0)),
            pl.BlockSpec((1, d), lambda i: (0, 0)),
            pl.BlockSpec((1, d), lambda i: (0, 0)),
            pl.BlockSpec((1, d), lambda i: (0, 0)),
            pl.BlockSpec((d, n), lambda i: (0, 0)),
        ],
        out_specs=[
            pl.BlockSpec((tm, _QW), lambda i: (i, 0)),
            pl.BlockSpec((tm, _KW), lambda i: (i, 0)),
            pl.BlockSpec((tm, _KW), lambda i: (i, 0)),
            pl.BlockSpec((tm, kvw), lambda i: (i, 0)),
            pl.BlockSpec((tm, kvw), lambda i: (i, 0)),
        ],
        out_shape=[
            jax.ShapeDtypeStruct((t, _QW), BF16),
            jax.ShapeDtypeStruct((t, _KW), BF16),
            jax.ShapeDtypeStruct((t, _KW), BF16),
            jax.ShapeDtypeStruct((t, kvw), F32),
            jax.ShapeDtypeStruct((t, kvw), F32),
        ],
        scratch_shapes=[pltpu.VMEM((MOBA_KV_HEADS, LANES, LANES), F32)],
        compiler_params=_cparams("arbitrary"),
        name="moba_qkv_gate",
    )(x, g, shift, scale, w)


def _moba_attn_kernel(slopes_ref, q_ref, k_ref, v_ref, al_ref, o_ref):
    kvh = pl.program_id(0)
    i = pl.program_id(1)
    tq = q_ref.shape[0]
    blk = MOBA_BLOCK
    row = lax.broadcasted_iota(jnp.int32, (tq, blk), 0)
    col = lax.broadcasted_iota(jnp.int32, (tq, blk), 1)
    causal = col <= row
    for g in range(MOBA_GROUP):
        hd = kvh * MOBA_GROUP + g
        q = q_ref[:, g * LANES:(g + 1) * LANES]
        slope = slopes_ref[hd]
        al = al_ref[pl.ds(hd, 1), :]

        def update(j, carry, masked):
            m, l, acc = carry
            start = pl.multiple_of(j * blk, blk)
            kb = k_ref[pl.ds(start, blk), :]
            vb = v_ref[pl.ds(start, blk), :]
            s = _nt_dot(q, kb) + al
            if masked:
                s = jnp.where(causal, s, NEG)
            c = slope * ((j - i) * blk).astype(F32)
            m_new = jnp.maximum(m, jnp.max(s, axis=-1, keepdims=True) + c)
            p = jnp.exp(s - (m_new - c))
            alpha = jnp.exp(m - m_new)
            l = alpha * l + jnp.sum(p, axis=-1, keepdims=True)
            acc = alpha * acc + jnp.dot(p.astype(BF16), vb, preferred_element_type=F32)
            return m_new, l, acc

        init = (jnp.full((tq, 1), NEG, F32), jnp.zeros((tq, 1), F32), jnp.zeros((tq, LANES), F32))
        carry = lax.fori_loop(0, i, functools.partial(update, masked=False), init)
        _, l, acc = update(i, carry, True)
        o_ref[:, g * LANES:(g + 1) * LANES] = (acc / l).astype(BF16)


def _moba_attn_call(q_aug, k_aug, v_pad, alibi, slopes):
    t = q_aug.shape[0]
    tq = MOBA_BLOCK
    gw = MOBA_GROUP * LANES
    return pl.pallas_call(
        _moba_attn_kernel,
        grid=(MOBA_KV_HEADS, t // tq),
        in_specs=[
            pl.BlockSpec(memory_space=pltpu.SMEM),
            pl.BlockSpec((tq, gw), lambda k, i: (i, k)),
            pl.BlockSpec((t, LANES), lambda k, i: (0, k)),
            pl.BlockSpec((t, LANES), lambda k, i: (0, k)),
            pl.BlockSpec((MOBA_HEADS, MOBA_BLOCK), lambda k, i: (0, 0)),
        ],
        out_specs=pl.BlockSpec((tq, gw), lambda k, i: (i, k)),
        out_shape=jax.ShapeDtypeStruct((t, _QW), BF16),
        compiler_params=_cparams("arbitrary", "arbitrary"),
        name="moba_attn",
    )(slopes, q_aug, k_aug, v_pad, alibi)


def _moe_kernel(x_ref, g_ref, sh_ref, sc_ref, gt_ref, gf_ref, wg_ref, we_ref, wgu_ref, wdn_ref, o_ref,
                h_sc, gate_sc, acc_sc, *, final):
    e = pl.program_id(1)
    tm = x_ref.shape[0]
    lane = lax.broadcasted_iota(jnp.int32, (tm, LANES), 1)

    @pl.when(e == 0)
    def _():
        h = _rms_mod(x_ref[...], g_ref[...], sh_ref[...], sc_ref[...])
        h_sc[...] = h.astype(BF16)
        lanef = lane.astype(F32)
        gl = jnp.dot(h, wg_ref[...], preferred_element_type=F32, precision=HIGHEST)
        gl = jnp.where(lane < N_GROUPS, gl, -jnp.inf)
        gmax = jnp.max(gl, axis=-1, keepdims=True)
        g_sel = jnp.min(jnp.where(gl == gmax, lanef, 1e9), axis=-1, keepdims=True)
        g_p = 1.0 / jnp.sum(jnp.exp(gl - gmax), axis=-1, keepdims=True)
        el = jnp.dot(h, we_ref[...], preferred_element_type=F32, precision=HIGHEST)
        lo = g_sel * EXPERTS_PER_GROUP
        el = jnp.where((lanef >= lo) & (lanef < lo + EXPERTS_PER_GROUP), el, -jnp.inf)
        emax = jnp.max(el, axis=-1, keepdims=True)
        ex = jnp.exp(el - emax)
        prob = ex / jnp.sum(ex, axis=-1, keepdims=True)
        prob = jnp.where(el > -jnp.inf, prob, -jnp.inf)
        p1 = jnp.max(prob, axis=-1, keepdims=True)
        i1 = jnp.min(jnp.where(prob == p1, lanef, 1e9), axis=-1, keepdims=True)
        prob2 = jnp.where(lanef == i1, -jnp.inf, prob)
        p2 = jnp.max(prob2, axis=-1, keepdims=True)
        i2 = jnp.min(jnp.where(prob2 == p2, lanef, 1e9), axis=-1, keepdims=True)
        den = p1 + p2
        gate_sc[...] = jnp.where(lanef == i1, p1 / den * g_p, jnp.where(lanef == i2, p2 / den * g_p, 0.0))
        acc_sc[...] = jnp.zeros_like(acc_sc)

    gcol = jnp.sum(jnp.where(lane == e, gate_sc[...], 0.0), axis=-1, keepdims=True)
    gu = jnp.dot(h_sc[...], wgu_ref[0], preferred_element_type=F32)
    gg = gu[:, :D_EXPERT]
    a = (gg * jax.nn.sigmoid(gg)) * gu[:, D_EXPERT:] * gcol
    acc_sc[...] += jnp.dot(a.astype(BF16), wdn_ref[0], preferred_element_type=F32)

    @pl.when(e == pl.num_programs(1) - 1)
    def _():
        y = x_ref[...] + gt_ref[...] * acc_sc[...]
        if final:
            y = y * lax.rsqrt(jnp.mean(y * y, axis=-1, keepdims=True) + NORM_EPS) * gf_ref[...]
        o_ref[...] = y


def _moe_call(x, g, shift, scale, gate, g_final, w_group, w_expert, w_gu, w_down, tm, final):
    m, d = x.shape
    tm = min(tm, m)
    ne, _, f2 = w_gu.shape
    fd = w_down.shape[1]
    return pl.pallas_call(
        functools.partial(_moe_kernel, final=final),
        grid=(m // tm, ne),
        in_specs=[
            pl.BlockSpec((tm, d), lambda i, e: (i, 0)),
            pl.BlockSpec((1, d), lambda i, e: (0, 0)),
            _mod_spec(shift, tm),
            _mod_spec(scale, tm),
            _mod_spec(gate, tm),
            pl.BlockSpec((1, d), lambda i, e: (0, 0)),
            pl.BlockSpec((d, LANES), lambda i, e: (0, 0)),
            pl.BlockSpec((d, LANES), lambda i, e: (0, 0)),
            pl.BlockSpec((1, d, f2), lambda i, e: (e, 0, 0)),
            pl.BlockSpec((1, fd, d), lambda i, e: (e, 0, 0)),
        ],
        out_specs=pl.BlockSpec((tm, d), lambda i, e: (i, 0)),
        out_shape=jax.ShapeDtypeStruct((m, d), F32),
        scratch_shapes=[pltpu.VMEM((tm, d), BF16), pltpu.VMEM((tm, LANES), F32), pltpu.VMEM((tm, d), F32)],
        compiler_params=_cparams("arbitrary", "arbitrary"),
        name="hier_moe",
    )(x, g, shift, scale, gate, g_final, w_group, w_expert, w_gu, w_down)


_MLA_W1 = MLA_Q_LORA + MLA_KV_LORA + 2 * LANES


def _mla_qkv_kernel(x_ref, g_ref, sh_ref, sc_ref, cos_ref, sin_ref, w1_ref, gq_ref, gkv_ref, wl_ref, wr_ref, wrr_ref,
                    q_ref, ckv_ref, kr_ref, kf_ref):
    h = _rms_mod(x_ref[...], g_ref[...], sh_ref[...], sc_ref[...]).astype(BF16)
    a = jnp.dot(h, w1_ref[...], preferred_element_type=F32)
    cos = cos_ref[...]
    sin = sin_ref[...]
    cq = a[:, :MLA_Q_LORA]
    cqn = (cq * lax.rsqrt(jnp.mean(cq * cq, axis=-1, keepdims=True) + NORM_EPS) * gq_ref[...]).astype(BF16)
    kvc = a[:, MLA_Q_LORA:MLA_Q_LORA + MLA_KV_LORA]
    ckv = kvc * lax.rsqrt(jnp.mean(kvc * kvc, axis=-1, keepdims=True) + NORM_EPS) * gkv_ref[...]
    o = MLA_Q_LORA + MLA_KV_LORA
    kr = a[:, o:o + LANES] * cos + a[:, o + LANES:o + 2 * LANES] * sin
    ckv_ref[...] = ckv
    kr_ref[...] = kr[:, :MLA_ROPE]
    kf_ref[:, :LANES] = ckv.astype(BF16)
    kf_ref[:, LANES:] = kr.astype(BF16)
    scale = (MLA_NOPE + MLA_ROPE) ** -0.5
    ql = jnp.dot(cqn, wl_ref[...], preferred_element_type=F32)
    qr = jnp.dot(cqn, wr_ref[...], preferred_element_type=F32)
    qrr = jnp.dot(cqn, wrr_ref[...], preferred_element_type=F32)
    for hd in range(MLA_HEADS):
        sl = slice(hd * LANES, (hd + 1) * LANES)
        q_ref[hd, :, :LANES] = (ql[:, sl] * scale).astype(BF16)
        q_ref[hd, :, LANES:] = ((qr[:, sl] * cos + qrr[:, sl] * sin) * scale).astype(BF16)


def _mla_qkv_call(x, g, shift, scale, cos, sin, w1, gq, gkv, wl, wr, wrr, tm):
    m, d = x.shape
    tm = min(tm, m)
    hw = MLA_HEADS * LANES
    full = lambda shape: pl.BlockSpec(shape, lambda i: (0,) * len(shape))
    return pl.pallas_call(
        _mla_qkv_kernel,
        grid=(m // tm,),
        in_specs=[
            pl.BlockSpec((tm, d), lambda i: (i, 0)),
            full((1, d)),
            _mod_spec(shift, tm),
            _mod_spec(scale, tm),
            pl.BlockSpec((tm, LANES), lambda i: (i, 0)),
            pl.BlockSpec((tm, LANES), lambda i: (i, 0)),
            full((d, _MLA_W1)),
            full((1, MLA_Q_LORA)),
            full((1, MLA_KV_LORA)),
            full((MLA_Q_LORA, hw)),
            full((MLA_Q_LORA, hw)),
            full((MLA_Q_LORA, hw)),
        ],
        out_specs=[
            pl.BlockSpec((MLA_HEADS, tm, 2 * LANES), lambda i: (0, i, 0)),
            pl.BlockSpec((tm, MLA_KV_LORA), lambda i: (i, 0)),
            pl.BlockSpec((tm, MLA_ROPE), lambda i: (i, 0)),
            pl.BlockSpec((tm, 2 * LANES), lambda i: (i, 0)),
        ],
        out_shape=[
            jax.ShapeDtypeStruct((MLA_HEADS, m, 2 * LANES), BF16),
            jax.ShapeDtypeStruct((m, MLA_KV_LORA), F32),
            jax.ShapeDtypeStruct((m, MLA_ROPE), F32),
            jax.ShapeDtypeStruct((m, 2 * LANES), BF16),
        ],
        compiler_params=_cparams("arbitrary"),
        name="mla_qkv",
    )(x, g, shift, scale, cos, sin, w1, gq, gkv, wl, wr, wrr)


def _mla_attn_kernel(q_ref, k_ref, o_ref, m_sc, l_sc, acc_sc):
    i = pl.program_id(0)
    nh, tq, _ = q_ref.shape
    row = lax.broadcasted_iota(jnp.int32, (tq, tq), 0)
    col = lax.broadcasted_iota(jnp.int32, (tq, tq), 1)
    causal = col <= row
    m_sc[...] = jnp.full_like(m_sc, NEG)
    l_sc[...] = jnp.zeros_like(l_sc)
    acc_sc[...] = jnp.zeros_like(acc_sc)

    def tile(j, masked):
        start = pl.multiple_of(j * tq, tq)
        kb = k_ref[pl.ds(start, tq), :]
        vb = kb[:, :LANES]

        def head(hd, _):
            s = _nt_dot(q_ref[hd], kb)
            if masked:
                s = jnp.where(causal, s, NEG)
            m = m_sc[hd]
            m_new = jnp.maximum(m, jnp.max(s, axis=-1, keepdims=True))
            p = jnp.exp(s - m_new)
            alpha = jnp.exp(m - m_new)
            l_sc[hd] = alpha * l_sc[hd] + jnp.sum(p, axis=-1, keepdims=True)
            acc_sc[hd] = alpha * acc_sc[hd] + jnp.dot(p.astype(BF16), vb, preferred_element_type=F32)
            m_sc[hd] = m_new
            return 0

        lax.fori_loop(0, nh, head, 0)

    def past(j, _):
        tile(j, False)
        return 0

    lax.fori_loop(0, i, past, 0)
    tile(i, True)
    for hd in range(nh):
        o_ref[:, hd * LANES:(hd + 1) * LANES] = (acc_sc[hd] / l_sc[hd]).astype(BF16)


def _mla_attn_call(q_full, k_full, tq):
    nh, t, w = q_full.shape
    return pl.pallas_call(
        _mla_attn_kernel,
        grid=(t // tq,),
        in_specs=[
            pl.BlockSpec((nh, tq, w), lambda i: (0, i, 0)),
            pl.BlockSpec((t, w), lambda i: (0, 0)),
        ],
        out_specs=pl.BlockSpec((tq, nh * LANES), lambda i: (i, 0)),
        out_shape=jax.ShapeDtypeStruct((t, nh * LANES), BF16),
        scratch_shapes=[pltpu.VMEM((nh, tq, 1), F32), pltpu.VMEM((nh, tq, 1), F32), pltpu.VMEM((nh, tq, LANES), F32)],
        compiler_params=_cparams("arbitrary"),
        name="mla_attn",
    )(q_full, k_full)


def _moba_sample_kernel(pt_ref, q_ref, slope_ref, tpos_ref, hk_ref, kn_ref, vn_ref, *rest, n_pages, past_len):
    pps = PAGES_PER_STEP
    kp = rest[:pps]
    vp = rest[pps:2 * pps]
    o_ref = rest[2 * pps]
    s_sc, p_sc, pn_sc, l_sc, acc_sc = rest[2 * pps + 1:]
    st = pl.program_id(1)
    n_chunks = n_pages // pps
    n_blocks = n_pages // 2
    q = q_ref[0]
    rows = q.shape[0]
    lane = lax.broadcasted_iota(jnp.int32, (rows, LANES), 1)
    lanef = lane.astype(F32)

    @pl.when(st < n_chunks)
    def _():
        for p in range(pps):
            s_sc[st * pps + p] = jnp.dot(q, kp[p][0].astype(BF16), preferred_element_type=F32)

    @pl.when(st == n_chunks - 1)
    def _():
        def block_sum(b, gs):
            r = jnp.sum(s_sc[2 * b] + s_sc[2 * b + 1], axis=-1, keepdims=True)
            return jnp.where(lane == b, r, gs)

        gs = lax.fori_loop(0, n_blocks, block_sum, jnp.zeros((rows, LANES), F32))
        selb = _top_mask_bias(jnp.where(lane < n_blocks, gs, -jnp.inf), lanef, min(MOBA_TOPK, n_blocks),
                              jnp.full((rows, LANES), NEG, F32))
        slope = slope_ref[...]
        tpos = tpos_ref[...]
        sn = _nt_dot(q, kn_ref[0]) - slope * (tpos - (past_len + lanef))
        sn = jnp.where(past_len + lanef <= tpos, sn, NEG)

        def scores(b, u):
            bcol = jnp.sum(jnp.where(lane == b, selb, 0.0), axis=-1, keepdims=True)
            pos = ((2 * b + u) * LANES).astype(F32) + lanef
            return s_sc[2 * b + u] + slope * (pos - tpos) + bcol

        def pass1(b, m):
            return jnp.maximum(m, jnp.maximum(scores(b, 0), scores(b, 1)))

        mv = lax.fori_loop(0, n_blocks, pass1, sn)
        m = jnp.max(mv, axis=-1, keepdims=True)

        def pass2(b, lv):
            for u in range(2):
                p = jnp.exp(scores(b, u) - m)
                p_sc[2 * b + u] = p.astype(BF16)
                lv = lv + p
            return lv

        pn = jnp.exp(sn - m)
        lv = lax.fori_loop(0, n_blocks, pass2, pn)
        l_sc[...] = jnp.sum(lv, axis=-1, keepdims=True)
        pn_sc[...] = pn.astype(BF16)
        acc_sc[...] = jnp.zeros_like(acc_sc)

    @pl.when(st >= n_chunks)
    def _():
        c = st - n_chunks
        acc = acc_sc[...]
        for p in range(pps):
            acc += _nt_dot(p_sc[c * pps + p], vp[p][0].astype(BF16))
        acc_sc[...] = acc

    @pl.when(st == 2 * n_chunks - 1)
    def _():
        acc = acc_sc[...] + jnp.dot(pn_sc[...], vn_ref[0], preferred_element_type=F32)
        o_ref[0] = (acc / l_sc[...] * hk_ref[...]).astype(BF16)


def _moba_sample_call(page_table, qbd, slope_col, tpos_col, hk_mask, k_new, v_new, cache_k, cache_v):
    nb, n_pages = page_table.shape
    rows, w = qbd.shape[1:]
    page = cache_k.shape[2]
    assert page == LANES and n_pages % PAGES_PER_STEP == 0
    n_chunks = n_pages // PAGES_PER_STEP
    pps = PAGES_PER_STEP

    def k_spec(p):
        return pl.BlockSpec((1, w, page), lambda b, s, pt: (pt[b, jnp.minimum(s, n_chunks - 1) * pps + p], 0, 0))

    def v_spec(p):
        return pl.BlockSpec((1, w, page), lambda b, s, pt: (pt[b, jnp.maximum(s - n_chunks, 0) * pps + p], 0, 0))

    per_seq = lambda shape: pl.BlockSpec((1,) + shape, lambda b, s, pt: (b, 0, 0))
    const = lambda shape: pl.BlockSpec(shape, lambda b, s, pt: (0, 0))
    grid_spec = pltpu.PrefetchScalarGridSpec(
        num_scalar_prefetch=1,
        grid=(nb, 2 * n_chunks),
        in_specs=[per_seq((rows, w)), const((rows, 1)), const((rows, 1)), const((rows, w)),
                  per_seq((LANES, w)), per_seq((LANES, w))]
        + [k_spec(p) for p in range(pps)] + [v_spec(p) for p in range(pps)],
        out_specs=per_seq((rows, w)),
        scratch_shapes=[
            pltpu.VMEM((n_pages, rows, LANES), F32),
            pltpu.VMEM((n_pages, rows, LANES), BF16),
            pltpu.VMEM((rows, LANES), BF16),
            pltpu.VMEM((rows, 1), F32),
            pltpu.VMEM((rows, w), F32),
        ],
    )
    return pl.pallas_call(
        functools.partial(_moba_sample_kernel, n_pages=n_pages, past_len=n_pages * page),
        grid_spec=grid_spec,
        out_shape=jax.ShapeDtypeStruct((nb, rows, w), BF16),
        compiler_params=_cparams("arbitrary", "arbitrary"),
        name="moba_sample_attn",
    )(page_table, qbd, slope_col, tpos_col, hk_mask, k_new, v_new, *([cache_k] * pps), *([cache_v] * pps))


def _mla_sample_kernel(pt_ref, q_ref, trow_ref, kn_ref, *rest, n_pages):
    pps = PAGES_PER_STEP
    cp = rest[:pps]
    rp = rest[pps:2 * pps]
    o_ref = rest[2 * pps]
    s_sc, m_sc, l_sc, acc_sc = rest[2 * pps + 1:]
    st = pl.program_id(1)
    q = q_ref[0]
    ql = q[:, :LANES]
    qr = q[:, LANES:LANES + MLA_ROPE]
    rows = q.shape[0]

    @pl.when(st == 0)
    def _():
        m_sc[...] = jnp.full_like(m_sc, NEG)
        l_sc[...] = jnp.zeros_like(l_sc)
        acc_sc[...] = jnp.zeros_like(acc_sc)

    mv = jnp.full((rows, LANES), NEG, F32)
    cbs = []
    for p in range(pps):
        cb = cp[p][0].astype(BF16)
        cbs.append(cb)
        s = _nt_dot(ql, cb) + jnp.dot(qr, rp[p][0].astype(BF16), preferred_element_type=F32)
        s_sc[p] = s
        mv = jnp.maximum(mv, s)
    m = m_sc[...]
    m_new = jnp.maximum(m, jnp.max(mv, axis=-1, keepdims=True))
    alpha = jnp.exp(m - m_new)
    lv = jnp.zeros((rows, LANES), F32)
    acc = alpha * acc_sc[...]
    for p in range(pps):
        pr = jnp.exp(s_sc[p] - m_new)
        lv = lv + pr
        acc = acc + jnp.dot(pr.astype(BF16), cbs[p], preferred_element_type=F32)
    l_sc[...] = alpha * l_sc[...] + jnp.sum(lv, axis=-1, keepdims=True)
    acc_sc[...] = acc
    m_sc[...] = m_new

    @pl.when(st == pl.num_programs(1) - 1)
    def _():
        kn = kn_ref[0]
        lanef = lax.broadcasted_iota(jnp.int32, (rows, LANES), 1).astype(F32)
        sn = jnp.where(lanef <= trow_ref[...], _nt_dot(q, kn), NEG)
        m1 = m_sc[...]
        m2 = jnp.maximum(m1, jnp.max(sn, axis=-1, keepdims=True))
        a2 = jnp.exp(m1 - m2)
        pn = jnp.exp(sn - m2)
        l = a2 * l_sc[...] + jnp.sum(pn, axis=-1, keepdims=True)
        acc2 = a2 * acc_sc[...] + jnp.dot(pn.astype(BF16), kn[:, :LANES], preferred_element_type=F32)
        o_ref[0] = (acc2 / l).astype(BF16)


def _mla_sample_call(page_table, q_rows, trow_col, k_new, cache_ckv, cache_krope):
    nb, n_pages = page_table.shape
    rows, w = q_rows.shape[1:]
    page = cache_ckv.shape[1]
    assert page == LANES and n_pages % PAGES_PER_STEP == 0
    pps = PAGES_PER_STEP

    def c_spec(p):
        return pl.BlockSpec((1, page, MLA_KV_LORA), lambda b, s, pt: (pt[b, s * pps + p], 0, 0))

    def r_spec(p):
        return pl.BlockSpec((1, MLA_ROPE, page), lambda b, s, pt: (pt[b, s * pps + p], 0, 0))

    per_seq = lambda shape: pl.BlockSpec((1,) + shape, lambda b, s, pt: (b, 0, 0))
    grid_spec = pltpu.PrefetchScalarGridSpec(
        num_scalar_prefetch=1,
        grid=(nb, n_pages // pps),
        in_specs=[per_seq((rows, w)), pl.BlockSpec((rows, 1), lambda b, s, pt: (0, 0)), per_seq((LANES, w))]
        + [c_spec(p) for p in range(pps)] + [r_spec(p) for p in range(pps)],
        out_specs=per_seq((rows, LANES)),
        scratch_shapes=[
            pltpu.VMEM((pps, rows, LANES), F32),
            pltpu.VMEM((rows, 1), F32),
            pltpu.VMEM((rows, 1), F32),
            pltpu.VMEM((rows, LANES), F32),
        ],
    )
    return pl.pallas_call(
        functools.partial(_mla_sample_kernel, n_pages=n_pages),
        grid_spec=grid_spec,
        out_shape=jax.ShapeDtypeStruct((nb, rows, LANES), BF16),
        compiler_params=_cparams("arbitrary", "arbitrary"),
        name="mla_sample_attn",
    )(page_table, q_rows, trow_col, k_new, *([cache_ckv] * pps), *([cache_krope] * pps))


def _pad_last(w, width):
    return jnp.pad(w, [(0, 0)] * (w.ndim - 1) + [(0, width - w.shape[-1])])


def _moba_weights(w_qkv, w_o):
    d = w_qkv.shape[0]
    nq = MOBA_HEADS * MOBA_HEAD_DIM
    nk = MOBA_KV_HEADS * MOBA_HEAD_DIM
    wq = _pad_last(w_qkv[:, :nq].reshape(d, MOBA_HEADS, MOBA_HEAD_DIM), LANES).reshape(d, _QW)
    wk = w_qkv[:, nq:nq + nk]
    wv = w_qkv[:, nq + nk:]
    wkp = _pad_last(wk.reshape(d, MOBA_KV_HEADS, MOBA_HEAD_DIM), LANES).reshape(d, _KW)
    wvp = _pad_last(wv.reshape(d, MOBA_KV_HEADS, MOBA_HEAD_DIM), LANES).reshape(d, _KW)
    w_fused = jnp.concatenate([wq, wkp, wvp, wk, wv], axis=1).astype(BF16)
    wo = w_o.reshape(MOBA_HEADS, MOBA_HEAD_DIM, d)
    wo_prompt = jnp.pad(wo, ((0, 0), (0, LANES - MOBA_HEAD_DIM), (0, 0))).reshape(_QW, d).astype(BF16)
    kv_of = jnp.arange(MOBA_HEADS) // MOBA_GROUP
    onehot = (kv_of[:, None] == jnp.arange(MOBA_KV_HEADS)[None, :]).astype(F32)
    wo_sample = (onehot[:, :, None, None] * wo[:, None, :, :]).reshape(MOBA_HEADS * nk, d).astype(BF16)
    return w_fused, wo_prompt, wo_sample


def _rotate_half_cols(w):
    half = w.shape[-1] // 2
    return jnp.concatenate([-w[..., half:], w[..., :half]], axis=-1)


def _mla_weights(w_dq, w_uq, w_dkv, w_uk, w_uv, w_o):
    d = w_dq.shape[0]
    w_r = w_dkv[:, MLA_KV_LORA:]
    w1 = jnp.concatenate([w_dq, w_dkv[:, :MLA_KV_LORA], _pad_last(w_r, LANES), _pad_last(_rotate_half_cols(w_r), LANES)],
                         axis=1).astype(BF16)
    wq = w_uq.reshape(MLA_Q_LORA, MLA_HEADS, MLA_NOPE + MLA_ROPE)
    wq_nope = wq[:, :, :MLA_NOPE].transpose(1, 0, 2)
    wuk_t = w_uk.transpose(1, 2, 0)
    w_lat = _bmm(wq_nope, wuk_t).transpose(1, 0, 2).reshape(MLA_Q_LORA, MLA_HEADS * LANES).astype(BF16)
    wq_rope = wq[:, :, MLA_NOPE:]
    w_qr = _pad_last(wq_rope, LANES).reshape(MLA_Q_LORA, MLA_HEADS * LANES).astype(BF16)
    w_qrr = _pad_last(_rotate_half_cols(wq_rope), LANES).reshape(MLA_Q_LORA, MLA_HEADS * LANES).astype(BF16)
    wuv = w_uv.transpose(1, 0, 2)
    wo = w_o.reshape(MLA_HEADS, MLA_V, d)
    w_vo = _bmm(wuv, wo).reshape(MLA_HEADS * MLA_KV_LORA, d).astype(BF16)
    return w1, w_lat, w_qr, w_qrr, w_vo


def _rope_tables(pos):
    inv = ROPE_THETA ** (-jnp.arange(0, MLA_ROPE, 2, dtype=F32) / MLA_ROPE)
    ang = pos.astype(F32)[:, None] * inv[None, :]
    cos = jnp.cos(ang)
    sin = jnp.sin(ang)
    return _pad_last(jnp.concatenate([cos, cos], axis=-1), LANES), _pad_last(jnp.concatenate([sin, sin], axis=-1), LANES)


def kernel(x_prompt, x_sample, cache_moba_k, cache_moba_v, cache_mla_ckv, cache_mla_krope, page_table, c_prompt, c_sample, w_ada, b_ada, g_norm1, g_norm2, moba_w_qkv, moba_w_o, mla_w_dq, mla_g_q, mla_w_uq, mla_w_dkv, mla_g_kv, mla_w_uk, mla_w_uv, mla_w_o, moe_w_group, moe_w_expert, moe_w_gu, moe_w_down, g_final):
    bp, seq, d = x_prompt.shape
    assert bp == 1
    nb, tdec, _ = x_sample.shape
    n_pool, page = cache_moba_k.shape[:2]
    n_pages = page_table.shape[1]
    past = n_pages * page
    ms = nb * tdec
    xp = x_prompt.reshape(seq, d)
    xs = x_sample.reshape(ms, d)

    c_all = jnp.concatenate([c_sample, c_prompt, jnp.zeros((7, d), F32)], axis=0)
    ada = _ada_call(c_all, w_ada, b_ada).reshape(w_ada.shape[0], c_all.shape[0], N_ADA, d)

    def mods(layer):
        mp = [ada[layer, nb:nb + 1, i] for i in range(N_ADA)]
        msm = [jnp.repeat(ada[layer, :nb, i], tdec, axis=0) for i in range(N_ADA)]
        return mp, msm

    row = lambda v: v.reshape(1, -1)
    wg = [_pad_last(moe_w_group[l], LANES) for l in range(2)]
    we = [_pad_last(moe_w_expert[l], LANES) for l in range(2)]
    wgu = moe_w_gu.astype(BF16)
    wdn = moe_w_down.astype(BF16)
    gfin = row(g_final)

    def moe(x, layer, m, tm, final):
        return _moe_call(x, row(g_norm2[layer]), m[3], m[4], m[5], gfin, wg[layer], we[layer], wgu[layer], wdn[layer],
                         tm, final)

    mp, msm = mods(0)
    w_fused, wo_prompt, wo_sample = _moba_weights(moba_w_qkv, moba_w_o)
    g1 = row(g_norm1[0])
    slopes = jnp.exp2(-8.0 * jnp.arange(1, MOBA_HEADS + 1, dtype=F32) / MOBA_HEADS)

    q_aug, k_aug, v_pad, k_out, v_out = _moba_qkv_call(xp, g1, mp[0], mp[1], w_fused)
    alibi = slopes[:, None] * jnp.arange(MOBA_BLOCK, dtype=F32)[None, :]
    o_pad = _moba_attn_call(q_aug, k_aug, v_pad, alibi, slopes)
    xp = _mm_res(o_pad, wo_prompt, xp, mp[2], 512)
    xp = moe(xp, 0, mp, 1024, False)
    moba_k_prompt = k_out.reshape(1, seq, MOBA_KV_HEADS, MOBA_HEAD_DIM)
    moba_v_prompt = v_out.reshape(1, seq, MOBA_KV_HEADS, MOBA_HEAD_DIM)

    nq = MOBA_HEADS * MOBA_HEAD_DIM
    nk = MOBA_KV_HEADS * MOBA_HEAD_DIM
    qkv_s = _norm_matmul(xs, g1, msm[0], msm[1], moba_w_qkv.astype(BF16), 256)
    q_s = qkv_s[:, :nq].reshape(nb, tdec, MOBA_HEADS, 1, MOBA_HEAD_DIM) * MOBA_HEAD_DIM ** -0.5
    k_s = qkv_s[:, nq:nq + nk]
    v_s = qkv_s[:, nq + nk:]
    kv_of = jnp.arange(MOBA_HEADS) // MOBA_GROUP
    head_kv = (kv_of[:, None] == jnp.arange(MOBA_KV_HEADS)[None, :]).astype(F32)
    rows = tdec * MOBA_HEADS
    qbd = (q_s * head_kv[None, None, :, :, None]).reshape(nb, rows, nk).astype(BF16)
    slope_col = jnp.tile(slopes, tdec).reshape(rows, 1)
    tpos_col = jnp.repeat(past + jnp.arange(tdec, dtype=F32), MOBA_HEADS).reshape(rows, 1)
    hk_mask = jnp.tile(jnp.repeat(head_kv, MOBA_HEAD_DIM, axis=1), (tdec, 1))
    pad_new = lambda a: jnp.pad(a.reshape(nb, tdec, -1), ((0, 0), (0, LANES - tdec), (0, 0))).astype(BF16)
    om = _moba_sample_call(page_table, qbd, slope_col, tpos_col, hk_mask, pad_new(k_s), pad_new(v_s),
                           cache_moba_k.transpose(0, 2, 3, 1).reshape(n_pool, nk, page),
                           cache_moba_v.transpose(0, 2, 3, 1).reshape(n_pool, nk, page))
    xs = _mm_res(om.reshape(ms, MOBA_HEADS * nk), wo_sample, xs, msm[2], 256)
    xs = moe(xs, 0, msm, 512, False)
    moba_k_sample = k_s.reshape(nb, tdec, MOBA_KV_HEADS, MOBA_HEAD_DIM)
    moba_v_sample = v_s.reshape(nb, tdec, MOBA_KV_HEADS, MOBA_HEAD_DIM)

    mp, msm = mods(1)
    w1, w_lat, w_qr, w_qrr, w_vo = _mla_weights(mla_w_dq, mla_w_uq, mla_w_dkv, mla_w_uk, mla_w_uv, mla_w_o)
    g1 = row(g_norm1[1])
    gq = row(mla_g_q)
    gkv = row(mla_g_kv)

    cos_p, sin_p = _rope_tables(jnp.arange(seq))
    q_full, ckv_p, kr_p, k_full = _mla_qkv_call(xp, g1, mp[0], mp[1], cos_p, sin_p, w1, gq, gkv, w_lat, w_qr, w_qrr, 256)
    o_lat = _mla_attn_call(q_full, k_full, 256)
    xp = _mm_res(o_lat, w_vo, xp, mp[2], 512)
    y_prompt = moe(xp, 1, mp, 1024, True).reshape(1, seq, d)

    cos_s, sin_s = _rope_tables(jnp.tile(past + jnp.arange(tdec), nb))
    q_fs, ckv_s, kr_s, k_fs = _mla_qkv_call(xs, g1, msm[0], msm[1], cos_s, sin_s, w1, gq, gkv, w_lat, w_qr, w_qrr, 256)
    q_rows = q_fs.reshape(MLA_HEADS, nb, tdec, 2 * LANES).transpose(1, 2, 0, 3).reshape(nb, tdec * MLA_HEADS, 2 * LANES)
    trow_col = jnp.repeat(jnp.arange(tdec, dtype=F32), MLA_HEADS).reshape(tdec * MLA_HEADS, 1)
    k_new = jnp.pad(k_fs.reshape(nb, tdec, 2 * LANES), ((0, 0), (0, LANES - tdec), (0, 0)))
    o_lat_s = _mla_sample_call(page_table, q_rows, trow_col, k_new, cache_mla_ckv, cache_mla_krope.transpose(0, 2, 1))
    xs = _mm_res(o_lat_s.reshape(ms, MLA_HEADS * LANES), w_vo, xs, msm[2], 256)
    y_sample = moe(xs, 1, msm, 512, True).reshape(nb, tdec, d)

    return (y_prompt, y_sample, moba_k_prompt, moba_v_prompt,
            ckv_p.reshape(1, seq, MLA_KV_LORA), kr_p.reshape(1, seq, MLA_ROPE),
            moba_k_sample, moba_v_sample,
            ckv_s.reshape(nb, tdec, MLA_KV_LORA), kr_s.reshape(nb, tdec, MLA_ROPE))
```

```python
import functools

import jax
import jax.numpy as jnp
from jax import lax
from jax.experimental import pallas as pl
from jax.experimental.pallas import tpu as pltpu

F32 = jnp.float32
BF16 = jnp.bfloat16
HIGHEST = lax.Precision.HIGHEST

D_MODEL = 1024
N_ADA = 6
NORM_EPS = 1e-6
MOBA_HEADS = 16
MOBA_KV_HEADS = 4
MOBA_GROUP = MOBA_HEADS // MOBA_KV_HEADS
MOBA_HEAD_DIM = 64
MOBA_BLOCK = 256
MOBA_TOPK = 3
MLA_HEADS = 16
MLA_Q_LORA = 256
MLA_KV_LORA = 128
MLA_NOPE = 64
MLA_ROPE = 32
MLA_V = 64
ROPE_THETA = 10000.0
N_GROUPS = 4
EXPERTS_PER_GROUP = 8
N_EXPERTS = N_GROUPS * EXPERTS_PER_GROUP
D_EXPERT = 256

LANES = 128
NEG = -1e30
LOG2E = 1.4426950408889634
VMEM_LIMIT = 56 * 1024 * 1024
MOBA_PAGES_PER_STEP = 32
PAGES_PER_STEP = 16

_NT = (((1,), (1,)), ((), ()))


def _nt_dot(a, b, **kw):
    return lax.dot_general(a, b, _NT, preferred_element_type=F32, **kw)


def _cparams(*sem):
    return pltpu.CompilerParams(dimension_semantics=sem, vmem_limit_bytes=VMEM_LIMIT)


def _rms_mod(x, g, shift, scale):
    y = x * lax.rsqrt(jnp.mean(x * x, axis=-1, keepdims=True) + NORM_EPS)
    return (y * g) * (1.0 + scale) + shift


def _top_mask_bias(vals, lanef, n_top, bias):
    for _ in range(n_top):
        m = jnp.max(vals, axis=-1, keepdims=True)
        idx = jnp.min(jnp.where(vals == m, lanef, 1e9), axis=-1, keepdims=True)
        idx = jnp.where(m > -jnp.inf, idx, -1.0)
        pick = lanef == idx
        bias = jnp.where(pick, 0.0, bias)
        vals = jnp.where(pick, -jnp.inf, vals)
    return bias


def _ada_kernel(c_ref, w_ref, b_ref, o_ref):
    c = c_ref[...]
    a = c * jax.nn.sigmoid(c)
    o_ref[0] = jnp.dot(a, w_ref[0], preferred_element_type=F32, precision=HIGHEST) + b_ref[0]


def _ada_call(c_all, w_ada, b_ada):
    depth, d, n = w_ada.shape
    rows = c_all.shape[0]
    tn = 512
    return pl.pallas_call(
        _ada_kernel,
        grid=(depth, n // tn),
        in_specs=[
            pl.BlockSpec((rows, d), lambda l, j: (0, 0)),
            pl.BlockSpec((1, d, tn), lambda l, j: (l, 0, j)),
            pl.BlockSpec((1, 1, tn), lambda l, j: (l, 0, j)),
        ],
        out_specs=pl.BlockSpec((1, rows, tn), lambda l, j: (l, 0, j)),
        out_shape=jax.ShapeDtypeStruct((depth, rows, n), F32),
        compiler_params=_cparams("arbitrary", "arbitrary"),
        name="ada_mod",
    )(c_all, w_ada, b_ada.reshape(depth, 1, n))


def _mod_spec(mod, tm):
    d = mod.shape[1]
    if mod.shape[0] == 1:
        return pl.BlockSpec((1, d), lambda i, *_: (0, 0))
    return pl.BlockSpec((tm, d), lambda i, *_: (i, 0))


def _norm_matmul_kernel(x_ref, g_ref, sh_ref, sc_ref, w_ref, o_ref):
    h = _rms_mod(x_ref[...], g_ref[...], sh_ref[...], sc_ref[...]).astype(BF16)
    o_ref[...] = jnp.dot(h, w_ref[...], preferred_element_type=F32)


def _norm_matmul(x, g, shift, scale, w, tm):
    m, d = x.shape
    tm = min(tm, m)
    n = w.shape[1]
    return pl.pallas_call(
        _norm_matmul_kernel,
        grid=(m // tm,),
        in_specs=[
            pl.BlockSpec((tm, d), lambda i: (i, 0)),
            pl.BlockSpec((1, d), lambda i: (0, 0)),
            _mod_spec(shift, tm),
            _mod_spec(scale, tm),
            pl.BlockSpec((d, n), lambda i: (0, 0)),
        ],
        out_specs=pl.BlockSpec((tm, n), lambda i: (i, 0)),
        out_shape=jax.ShapeDtypeStruct((m, n), F32),
        compiler_params=_cparams("arbitrary"),
        name="norm_matmul",
    )(x, g, shift, scale, w)


def _mm_res_kernel(a_ref, w_ref, x_ref, gate_ref, o_ref):
    y = jnp.dot(a_ref[...], w_ref[...], preferred_element_type=F32)
    o_ref[...] = x_ref[...] + gate_ref[...] * y


def _mm_res(a, w, x, gate, tm):
    m, k = a.shape
    tm = min(tm, m)
    n = w.shape[1]
    return pl.pallas_call(
        _mm_res_kernel,
        grid=(m // tm,),
        in_specs=[
            pl.BlockSpec((tm, k), lambda i: (i, 0)),
            pl.BlockSpec((k, n), lambda i: (0, 0)),
            pl.BlockSpec((tm, n), lambda i: (i, 0)),
            _mod_spec(gate, tm),
        ],
        out_specs=pl.BlockSpec((tm, n), lambda i: (i, 0)),
        out_shape=jax.ShapeDtypeStruct((m, n), F32),
        compiler_params=_cparams("arbitrary"),
        name="matmul_residual",
    )(a, w, x, gate)


def _bmm_kernel(a_ref, b_ref, o_ref):
    o_ref[0] = jnp.dot(a_ref[0], b_ref[0], preferred_element_type=F32, precision=HIGHEST)


def _bmm(a, b):
    n, m, k = a.shape
    p = b.shape[2]
    return pl.pallas_call(
        _bmm_kernel,
        grid=(n,),
        in_specs=[pl.BlockSpec((1, m, k), lambda i: (i, 0, 0)), pl.BlockSpec((1, k, p), lambda i: (i, 0, 0))],
        out_specs=pl.BlockSpec((1, m, p), lambda i: (i, 0, 0)),
        out_shape=jax.ShapeDtypeStruct((n, m, p), F32),
        compiler_params=_cparams("arbitrary"),
        name="fold_weights",
    )(a, b)


_QW = MOBA_HEADS * LANES
_KW = MOBA_KV_HEADS * LANES


def _moba_qkv_kernel(x_ref, g_ref, sh_ref, sc_ref, w_ref, q_ref, ka_ref, vp_ref, ko_ref, vo_ref, km_ref):
    i = pl.program_id(0)
    tm = x_ref.shape[0]
    half = LANES // 2

    @pl.when(i == 0)
    def _():
        km_ref[...] = jnp.zeros_like(km_ref)

    h = _rms_mod(x_ref[...], g_ref[...], sh_ref[...], sc_ref[...]).astype(BF16)
    y = jnp.dot(h, w_ref[...], preferred_element_type=F32)
    lane = lax.broadcasted_iota(jnp.int32, (tm, LANES), 1)
    lanef = lane.astype(F32)
    kvw = MOBA_KV_HEADS * MOBA_HEAD_DIM
    ko_ref[...] = y[:, _QW + 2 * _KW:_QW + 2 * _KW + kvw]
    vo_ref[...] = y[:, _QW + 2 * _KW + kvw:_QW + 2 * _KW + 2 * kvw]
    for k in range(MOBA_KV_HEADS):
        kp = y[:, _QW + k * LANES:_QW + (k + 1) * LANES]
        km_ref[k, pl.ds(half + i, 1), :] = jnp.mean(kp, axis=0, keepdims=True)
        ka_ref[:, k * LANES:(k + 1) * LANES] = jnp.where(lane == half + i, 1.0, kp).astype(BF16)
        vp = y[:, _QW + _KW + k * LANES:_QW + _KW + (k + 1) * LANES]
        vp_ref[:, k * LANES:(k + 1) * LANES] = jnp.where(lane == LANES - 1, 1.0, vp).astype(BF16)
    past = (lane >= half) & (lane < half + i)
    own_bias = jnp.where(lane == half + i, 0.0, NEG)
    for hd in range(MOBA_HEADS):
        qp = y[:, hd * LANES:(hd + 1) * LANES]
        gs = _nt_dot(qp, km_ref[hd // MOBA_GROUP], precision=HIGHEST)
        bias = _top_mask_bias(jnp.where(past, gs, -jnp.inf), lanef, MOBA_TOPK, own_bias)
        q_ref[hd] = jnp.where(lane < half, qp * (MOBA_HEAD_DIM ** -0.5 * LOG2E), bias).astype(BF16)


def _moba_qkv_call(x, g, shift, scale, w):
    t, d = x.shape
    tm = MOBA_BLOCK
    assert t // tm <= LANES // 2
    n = w.shape[1]
    kvw = MOBA_KV_HEADS * MOBA_HEAD_DIM
    return pl.pallas_call(
        _moba_qkv_kernel,
        grid=(t // tm,),
        in_specs=[
            pl.BlockSpec((tm, d), lambda i: (i, 0)),
            pl.BlockSpec((1, d), lambda i: (0, 0)),
            pl.BlockSpec((1, d), lambda i: (0, 0)),
            pl.BlockSpec((1, d), lambda i: (0, 0)),
            pl.BlockSpec((d, n), lambda i: (0, 0)),
        ],
        out_specs=[
            pl.BlockSpec((MOBA_HEADS, tm, LANES), lambda i: (0, i, 0)),
            pl.BlockSpec((tm, _KW), lambda i: (i, 0)),
            pl.BlockSpec((tm, _KW), lambda i: (i, 0)),
            pl.BlockSpec((tm, kvw), lambda i: (i, 0)),
            pl.BlockSpec((tm, kvw), lambda i: (i, 0)),
        ],
        out_shape=[
            jax.ShapeDtypeStruct((MOBA_HEADS, t, LANES), BF16),
            jax.ShapeDtypeStruct((t, _KW), BF16),
            jax.ShapeDtypeStruct((t, _KW), BF16),
            jax.ShapeDtypeStruct((t, kvw), F32),
            jax.ShapeDtypeStruct((t, kvw), F32),
        ],
        scratch_shapes=[pltpu.VMEM((MOBA_KV_HEADS, LANES, LANES), F32)],
        compiler_params=_cparams("arbitrary"),
        name="moba_qkv_gate",
    )(x, g, shift, scale, w)


def _moba_attn_kernel(q_ref, k_ref, v_ref, al_ref, sl_ref, o_ref, m_sc, acc_sc):
    i = pl.program_id(1)
    tq = q_ref.shape[1]
    blk = MOBA_BLOCK
    n_chains, rows, _ = m_sc.shape
    hpc = rows // tq
    m_sc[...] = jnp.full_like(m_sc, NEG)
    acc_sc[...] = jnp.zeros_like(acc_sc)

    def step(j, masked):
        start = pl.multiple_of(j * blk, blk)
        kb = k_ref[pl.ds(start, blk), :]
        vb = v_ref[pl.ds(start, blk), :]
        dj = ((j - i) * blk).astype(F32)
        for u in range(n_chains):
            q = q_ref[u * hpc:(u + 1) * hpc].reshape(rows, LANES)
            s = _nt_dot(q, kb) + al_ref[u]
            if masked:
                row_in = lax.broadcasted_iota(jnp.int32, (rows, blk), 0) & (tq - 1)
                col = lax.broadcasted_iota(jnp.int32, (rows, blk), 1)
                s = jnp.where(col <= row_in, s, NEG)
            c = sl_ref[u] * dj
            m_prev = m_sc[u]
            m_new = jnp.maximum(m_prev, jnp.max(s, axis=-1, keepdims=True) + c)
            shift = m_new - c
            p = jnp.exp2(s - jnp.concatenate([shift] * (blk // LANES), axis=1))
            acc_sc[u] = acc_sc[u] * jnp.exp2(m_prev - m_new) + jnp.dot(p.astype(BF16), vb, preferred_element_type=F32)
            m_sc[u] = m_new

    def past(j, _):
        step(j, False)
        return 0

    lax.fori_loop(0, i, past, 0)
    step(i, True)
    for u in range(n_chains):
        acc = acc_sc[u]
        o = acc / acc[:, LANES - 1:]
        for x in range(hpc):
            hd = u * hpc + x
            o_ref[:, hd * LANES:(hd + 1) * LANES] = o[x * tq:(x + 1) * tq].astype(BF16)


def _moba_attn_call(q_aug, k_aug, v_pad, al_tab, sl_tab):
    nh, t, _ = q_aug.shape
    tq = MOBA_BLOCK
    assert tq & (tq - 1) == 0
    n_chains, rows = al_tab.shape[0] // MOBA_KV_HEADS, al_tab.shape[1]
    gw = MOBA_GROUP * LANES
    return pl.pallas_call(
        _moba_attn_kernel,
        grid=(MOBA_KV_HEADS, t // tq),
        in_specs=[
            pl.BlockSpec((MOBA_GROUP, tq, LANES), lambda k, i: (k, i, 0)),
            pl.BlockSpec((t, LANES), lambda k, i: (0, k)),
            pl.BlockSpec((t, LANES), lambda k, i: (0, k)),
            pl.BlockSpec((n_chains, rows, MOBA_BLOCK), lambda k, i: (k, 0, 0)),
            pl.BlockSpec((n_chains, rows, LANES), lambda k, i: (k, 0, 0)),
        ],
        out_specs=pl.BlockSpec((tq, gw), lambda k, i: (i, k)),
        out_shape=jax.ShapeDtypeStruct((t, nh * LANES), BF16),
        scratch_shapes=[pltpu.VMEM((n_chains, rows, LANES), F32), pltpu.VMEM((n_chains, rows, LANES), F32)],
        compiler_params=_cparams("arbitrary", "arbitrary"),
        name="moba_attn",
    )(q_aug, k_aug, v_pad, al_tab, sl_tab)


def _moe_kernel(x_ref, g_ref, sh_ref, sc_ref, gt_ref, gf_ref, wg_ref, we_ref, wgu_ref, wdn_ref, o_ref,
                h_sc, gate_sc, acc_sc, *, final):
    e = pl.program_id(1)
    tm = x_ref.shape[0]
    lane = lax.broadcasted_iota(jnp.int32, (tm, LANES), 1)

    @pl.when(e == 0)
    def _():
        h = _rms_mod(x_ref[...], g_ref[...], sh_ref[...], sc_ref[...])
        h_sc[...] = h.astype(BF16)
        lanef = lane.astype(F32)
        gl = jnp.dot(h, wg_ref[...], preferred_element_type=F32, precision=HIGHEST)
        gl = jnp.where(lane < N_GROUPS, gl, -jnp.inf)
        gmax = jnp.max(gl, axis=-1, keepdims=True)
        g_sel = jnp.min(jnp.where(gl == gmax, lanef, 1e9), axis=-1, keepdims=True)
        g_p = 1.0 / jnp.sum(jnp.exp(gl - gmax), axis=-1, keepdims=True)
        el = jnp.dot(h, we_ref[...], preferred_element_type=F32, precision=HIGHEST)
        lo = g_sel * EXPERTS_PER_GROUP
        el = jnp.where((lanef >= lo) & (lanef < lo + EXPERTS_PER_GROUP), el, -jnp.inf)
        emax = jnp.max(el, axis=-1, keepdims=True)
        ex = jnp.exp(el - emax)
        prob = ex / jnp.sum(ex, axis=-1, keepdims=True)
        prob = jnp.where(el > -jnp.inf, prob, -jnp.inf)
        p1 = jnp.max(prob, axis=-1, keepdims=True)
        i1 = jnp.min(jnp.where(prob == p1, lanef, 1e9), axis=-1, keepdims=True)
        prob2 = jnp.where(lanef == i1, -jnp.inf, prob)
        p2 = jnp.max(prob2, axis=-1, keepdims=True)
        i2 = jnp.min(jnp.where(prob2 == p2, lanef, 1e9), axis=-1, keepdims=True)
        den = p1 + p2
        gate_sc[...] = jnp.where(lanef == i1, p1 / den * g_p, jnp.where(lanef == i2, p2 / den * g_p, 0.0))
        acc_sc[...] = jnp.zeros_like(acc_sc)

    gcol = jnp.sum(jnp.where(lane == e, gate_sc[...], 0.0), axis=-1, keepdims=True)
    gu = jnp.dot(h_sc[...], wgu_ref[0], preferred_element_type=F32)
    gg = gu[:, :D_EXPERT]
    a = (gg * jax.nn.sigmoid(gg)) * gu[:, D_EXPERT:] * gcol
    acc_sc[...] += jnp.dot(a.astype(BF16), wdn_ref[0], preferred_element_type=F32)

    @pl.when(e == pl.num_programs(1) - 1)
    def _():
        y = x_ref[...] + gt_ref[...] * acc_sc[...]
        if final:
            y = y * lax.rsqrt(jnp.mean(y * y, axis=-1, keepdims=True) + NORM_EPS) * gf_ref[...]
        o_ref[...] = y


def _moe_call(x, g, shift, scale, gate, g_final, w_group, w_expert, w_gu, w_down, tm, final):
    m, d = x.shape
    tm = min(tm, m)
    ne, _, f2 = w_gu.shape
    fd = w_down.shape[1]
    return pl.pallas_call(
        functools.partial(_moe_kernel, final=final),
        grid=(m // tm, ne),
        in_specs=[
            pl.BlockSpec((tm, d), lambda i, e: (i, 0)),
            pl.BlockSpec((1, d), lambda i, e: (0, 0)),
            _mod_spec(shift, tm),
            _mod_spec(scale, tm),
            _mod_spec(gate, tm),
            pl.BlockSpec((1, d), lambda i, e: (0, 0)),
            pl.BlockSpec((d, LANES), lambda i, e: (0, 0)),
            pl.BlockSpec((d, LANES), lambda i, e: (0, 0)),
            pl.BlockSpec((1, d, f2), lambda i, e: (e, 0, 0)),
            pl.BlockSpec((1, fd, d), lambda i, e: (e, 0, 0)),
        ],
        out_specs=pl.BlockSpec((tm, d), lambda i, e: (i, 0)),
        out_shape=jax.ShapeDtypeStruct((m, d), F32),
        scratch_shapes=[pltpu.VMEM((tm, d), BF16), pltpu.VMEM((tm, LANES), F32), pltpu.VMEM((tm, d), F32)],
        compiler_params=_cparams("arbitrary", "arbitrary"),
        name="hier_moe",
    )(x, g, shift, scale, gate, g_final, w_group, w_expert, w_gu, w_down)


_MLA_W1 = MLA_Q_LORA + MLA_KV_LORA + 2 * LANES


def _mla_qkv_kernel(x_ref, g_ref, sh_ref, sc_ref, cos_ref, sin_ref, w1_ref, gq_ref, gkv_ref, wl_ref, wr_ref, wrr_ref,
                    q_ref, ckv_ref, kr_ref, kf_ref):
    h = _rms_mod(x_ref[...], g_ref[...], sh_ref[...], sc_ref[...]).astype(BF16)
    a = jnp.dot(h, w1_ref[...], preferred_element_type=F32)
    cos = cos_ref[...]
    sin = sin_ref[...]
    cq = a[:, :MLA_Q_LORA]
    cqn = (cq * lax.rsqrt(jnp.mean(cq * cq, axis=-1, keepdims=True) + NORM_EPS) * gq_ref[...]).astype(BF16)
    kvc = a[:, MLA_Q_LORA:MLA_Q_LORA + MLA_KV_LORA]
    ckv = kvc * lax.rsqrt(jnp.mean(kvc * kvc, axis=-1, keepdims=True) + NORM_EPS) * gkv_ref[...]
    o = MLA_Q_LORA + MLA_KV_LORA
    kr = a[:, o:o + LANES] * cos + a[:, o + LANES:o + 2 * LANES] * sin
    ckv_ref[...] = ckv
    kr_ref[...] = kr[:, :MLA_ROPE]
    kf_ref[:, :LANES] = ckv.astype(BF16)
    lane = lax.broadcasted_iota(jnp.int32, kr.shape, 1)
    kf_ref[:, LANES:] = jnp.where(lane == LANES - 1, 1.0, kr).astype(BF16)
    scale = (MLA_NOPE + MLA_ROPE) ** -0.5 * LOG2E
    ql = jnp.dot(cqn, wl_ref[...], preferred_element_type=F32)
    qr = jnp.dot(cqn, wr_ref[...], preferred_element_type=F32)
    qrr = jnp.dot(cqn, wrr_ref[...], preferred_element_type=F32)
    for hd in range(MLA_HEADS):
        sl = slice(hd * LANES, (hd + 1) * LANES)
        q_ref[hd, :, :LANES] = (ql[:, sl] * scale).astype(BF16)
        q_ref[hd, :, LANES:] = ((qr[:, sl] * cos + qrr[:, sl] * sin) * scale).astype(BF16)


def _mla_qkv_call(x, g, shift, scale, cos, sin, w1, gq, gkv, wl, wr, wrr, tm):
    m, d = x.shape
    tm = min(tm, m)
    hw = MLA_HEADS * LANES
    full = lambda shape: pl.BlockSpec(shape, lambda i: (0,) * len(shape))
    return pl.pallas_call(
        _mla_qkv_kernel,
        grid=(m // tm,),
        in_specs=[
            pl.BlockSpec((tm, d), lambda i: (i, 0)),
            full((1, d)),
            _mod_spec(shift, tm),
            _mod_spec(scale, tm),
            pl.BlockSpec((tm, LANES), lambda i: (i, 0)),
            pl.BlockSpec((tm, LANES), lambda i: (i, 0)),
            full((d, _MLA_W1)),
            full((1, MLA_Q_LORA)),
            full((1, MLA_KV_LORA)),
            full((MLA_Q_LORA, hw)),
            full((MLA_Q_LORA, hw)),
            full((MLA_Q_LORA, hw)),
        ],
        out_specs=[
            pl.BlockSpec((MLA_HEADS, tm, 2 * LANES), lambda i: (0, i, 0)),
            pl.BlockSpec((tm, MLA_KV_LORA), lambda i: (i, 0)),
            pl.BlockSpec((tm, MLA_ROPE), lambda i: (i, 0)),
            pl.BlockSpec((tm, 2 * LANES), lambda i: (i, 0)),
        ],
        out_shape=[
            jax.ShapeDtypeStruct((MLA_HEADS, m, 2 * LANES), BF16),
            jax.ShapeDtypeStruct((m, MLA_KV_LORA), F32),
            jax.ShapeDtypeStruct((m, MLA_ROPE), F32),
            jax.ShapeDtypeStruct((m, 2 * LANES), BF16),
        ],
        compiler_params=_cparams("arbitrary"),
        name="mla_qkv",
    )(x, g, shift, scale, cos, sin, w1, gq, gkv, wl, wr, wrr)


def _mla_attn_kernel(q_ref, k_ref, o_ref, m_sc, acc_sc, *, tk, hpc):
    i = pl.program_id(0)
    nh, tq, w = q_ref.shape
    rows = hpc * tq
    n_chains = nh // hpc
    n_past = (i * tq) // tk
    m_sc[...] = jnp.full_like(m_sc, NEG)
    acc_sc[...] = jnp.zeros_like(acc_sc)

    def tile(j, masked):
        start = pl.multiple_of(j * tk, tk)
        kb = k_ref[pl.ds(start, tk), :]

        def chain_pair(cp, _):
            for u in range(2):
                c = cp * 2 + u
                q = q_ref[pl.ds(c * hpc, hpc)].reshape(rows, w)
                s = _nt_dot(q, kb)
                if masked:
                    row_pos = i * tq + (lax.broadcasted_iota(jnp.int32, (rows, tk), 0) & (tq - 1))
                    col_pos = start + lax.broadcasted_iota(jnp.int32, (rows, tk), 1)
                    s = jnp.where(col_pos <= row_pos, s, NEG)
                m_prev = m_sc[c]
                m_new = jnp.maximum(m_prev, jnp.max(s, axis=-1, keepdims=True))
                p = jnp.exp2(s - jnp.concatenate([m_new] * (tk // LANES), axis=1))
                alpha = jnp.exp2(m_prev - m_new)
                acc_sc[c] = (acc_sc[c] * jnp.concatenate([alpha] * (w // LANES), axis=1)
                             + jnp.dot(p.astype(BF16), kb, preferred_element_type=F32))
                m_sc[c] = m_new
            return 0

        lax.fori_loop(0, n_chains // 2, chain_pair, 0)

    def past(j, _):
        tile(j, False)
        return 0

    lax.fori_loop(0, n_past, past, 0)
    tile(n_past, True)
    for c in range(n_chains):
        acc = acc_sc[c]
        o = acc[:, :LANES] / acc[:, w - 1:]
        for x in range(hpc):
            hd = c * hpc + x
            o_ref[:, hd * LANES:(hd + 1) * LANES] = o[x * tq:(x + 1) * tq].astype(BF16)


def _mla_attn_call(q_full, k_full, tq, tk, hpc):
    nh, t, w = q_full.shape
    assert tq & (tq - 1) == 0 and tk % tq == 0 and t % tk == 0 and nh % (2 * hpc) == 0
    n_chains, rows = nh // hpc, hpc * tq
    return pl.pallas_call(
        functools.partial(_mla_attn_kernel, tk=tk, hpc=hpc),
        grid=(t // tq,),
        in_specs=[
            pl.BlockSpec((nh, tq, w), lambda i: (0, i, 0)),
            pl.BlockSpec((t, w), lambda i: (0, 0)),
        ],
        out_specs=pl.BlockSpec((tq, nh * LANES), lambda i: (i, 0)),
        out_shape=jax.ShapeDtypeStruct((t, nh * LANES), BF16),
        scratch_shapes=[pltpu.VMEM((n_chains, rows, LANES), F32), pltpu.VMEM((n_chains, rows, w), F32)],
        compiler_params=_cparams("arbitrary"),
        name="mla_attn",
    )(q_full, k_full)


def _moba_sample_kernel(pt_ref, q_ref, slope_ref, tpos_ref, hk_ref, kn_ref, vn_ref, *rest, n_pages, past_len):
    pps = MOBA_PAGES_PER_STEP
    kp = rest[:pps]
    vp = rest[pps:2 * pps]
    o_ref = rest[2 * pps]
    s_sc, p_sc, g_sc, b_sc, pn_sc, l_sc, acc_sc = rest[2 * pps + 1:]
    st = pl.program_id(1)
    n_chunks = n_pages // pps
    n_blocks = n_pages // 2
    q = q_ref[0]
    rows = q.shape[0]
    lane = lax.broadcasted_iota(jnp.int32, (rows, LANES), 1)
    lanef = lane.astype(F32)

    @pl.when(st < n_chunks)
    def _():
        for a in range(pps // 2):
            kb = jnp.concatenate([kp[2 * a][0].astype(BF16), kp[2 * a + 1][0].astype(BF16)], axis=1)
            s2 = jnp.dot(q, kb, preferred_element_type=F32)
            s_sc[st * pps + 2 * a] = s2[:, :LANES]
            s_sc[st * pps + 2 * a + 1] = s2[:, LANES:]

    @pl.when(st == n_chunks - 1)
    def _():
        for b in range(n_blocks):
            r = jnp.sum(s_sc[2 * b] + s_sc[2 * b + 1], axis=-1, keepdims=True)
            g_sc[b] = jnp.broadcast_to(r, (rows, LANES))
            b_sc[b] = jnp.full((rows, LANES), NEG, F32)
        for _ in range(min(MOBA_TOPK, n_blocks)):
            best = g_sc[0]
            for b in range(1, n_blocks):
                best = jnp.maximum(best, g_sc[b])
            idx = jnp.full((rows, LANES), float(n_blocks), F32)
            for b in reversed(range(n_blocks)):
                idx = jnp.where(g_sc[b] == best, float(b), idx)
            for b in range(n_blocks):
                hit = idx == float(b)
                b_sc[b] = jnp.where(hit, 0.0, b_sc[b])
                g_sc[b] = jnp.where(hit, -jnp.inf, g_sc[b])
        slope = jnp.broadcast_to(slope_ref[...], (rows, LANES))
        tpos = jnp.broadcast_to(tpos_ref[...], (rows, LANES))
        al_lane = slope * lanef
        sn = _nt_dot(q, kn_ref[0]) - slope * (tpos - (past_len + lanef))
        sn = jnp.where(past_len + lanef <= tpos, sn, NEG)
        mv = sn
        for b in range(n_blocks):
            for u in range(2):
                pg = 2 * b + u
                sc = s_sc[pg] + al_lane + (b_sc[b] + slope * (float(pg * LANES) - tpos))
                s_sc[pg] = sc
                mv = jnp.maximum(mv, sc)
        m = jnp.broadcast_to(jnp.max(mv, axis=-1, keepdims=True), (rows, LANES))
        pn = jnp.exp(sn - m)
        lv = pn
        for pg in range(n_pages):
            p = jnp.exp(s_sc[pg] - m)
            p_sc[pg] = p.astype(BF16)
            lv = lv + p
        l_sc[...] = jnp.sum(lv, axis=-1, keepdims=True)
        pn_sc[...] = pn.astype(BF16)
        acc_sc[...] = jnp.zeros_like(acc_sc)

    @pl.when(st >= n_chunks)
    def _():
        c = st - n_chunks
        acc = acc_sc[...]
        for a in range(pps // 2):
            pg = c * pps + 2 * a
            p2 = jnp.concatenate([p_sc[pg], p_sc[pg + 1]], axis=1)
            vt = jnp.concatenate([vp[2 * a][0].astype(BF16), vp[2 * a + 1][0].astype(BF16)], axis=1)
            acc += _nt_dot(p2, vt)
        acc_sc[...] = acc

    @pl.when(st == 2 * n_chunks - 1)
    def _():
        acc = acc_sc[...] + jnp.dot(pn_sc[...], vn_ref[0], preferred_element_type=F32)
        o_ref[0] = (acc / l_sc[...] * hk_ref[...]).astype(BF16)


def _moba_sample_call(page_table, qbd, slope_col, tpos_col, hk_mask, k_new, v_new, cache_k, cache_v):
    nb, n_pages = page_table.shape
    rows, w = qbd.shape[1:]
    page = cache_k.shape[2]
    pps = MOBA_PAGES_PER_STEP
    assert page == LANES and n_pages % pps == 0 and pps % 2 == 0
    n_chunks = n_pages // pps

    def k_spec(p):
        return pl.BlockSpec((1, w, page), lambda b, s, pt: (pt[b, jnp.minimum(s, n_chunks - 1) * pps + p], 0, 0))

    def v_spec(p):
        return pl.BlockSpec((1, w, page), lambda b, s, pt: (pt[b, jnp.maximum(s - n_chunks, 0) * pps + p], 0, 0))

    per_seq = lambda shape: pl.BlockSpec((1,) + shape, lambda b, s, pt: (b, 0, 0))
    const = lambda shape: pl.BlockSpec(shape, lambda b, s, pt: (0, 0))
    grid_spec = pltpu.PrefetchScalarGridSpec(
        num_scalar_prefetch=1,
        grid=(nb, 2 * n_chunks),
        in_specs=[per_seq((rows, w)), const((rows, 1)), const((rows, 1)), const((rows, w)),
                  per_seq((LANES, w)), per_seq((LANES, w))]
        + [k_spec(p) for p in range(pps)] + [v_spec(p) for p in range(pps)],
        out_specs=per_seq((rows, w)),
        scratch_shapes=[
            pltpu.VMEM((n_pages, rows, LANES), F32),
            pltpu.VMEM((n_pages, rows, LANES), BF16),
            pltpu.VMEM((n_pages // 2, rows, LANES), F32),
            pltpu.VMEM((n_pages // 2, rows, LANES), F32),
            pltpu.VMEM((rows, LANES), BF16),
            pltpu.VMEM((rows, 1), F32),
            pltpu.VMEM((rows, w), F32),
        ],
    )
    return pl.pallas_call(
        functools.partial(_moba_sample_kernel, n_pages=n_pages, past_len=n_pages * page),
        grid_spec=grid_spec,
        out_shape=jax.ShapeDtypeStruct((nb, rows, w), BF16),
        compiler_params=_cparams("arbitrary", "arbitrary"),
        name="moba_sample_attn",
    )(page_table, qbd, slope_col, tpos_col, hk_mask, k_new, v_new, *([cache_k] * pps), *([cache_v] * pps))


def _mla_sample_kernel(pt_ref, q_ref, trow_ref, kn_ref, *rest, n_pages):
    pps = PAGES_PER_STEP
    cp = rest[:pps]
    rp = rest[pps:2 * pps]
    o_ref = rest[2 * pps]
    s_sc, m_sc, l_sc, acc_sc = rest[2 * pps + 1:]
    st = pl.program_id(1)
    q = q_ref[0]
    ql = q[:, :LANES]
    qr = q[:, LANES:LANES + MLA_ROPE]
    rows = q.shape[0]

    @pl.when(st == 0)
    def _():
        m_sc[...] = jnp.full_like(m_sc, NEG)
        l_sc[...] = jnp.zeros_like(l_sc)
        acc_sc[...] = jnp.zeros_like(acc_sc)

    mv = jnp.full((rows, 2 * LANES), NEG, F32)
    cbs = []
    for a in range(pps // 2):
        cb = jnp.concatenate([cp[2 * a][0].astype(BF16), cp[2 * a + 1][0].astype(BF16)], axis=0)
        rb = jnp.concatenate([rp[2 * a][0].astype(BF16), rp[2 * a + 1][0].astype(BF16)], axis=1)
        cbs.append(cb)
        s = _nt_dot(ql, cb) + jnp.dot(qr, rb, preferred_element_type=F32)
        s_sc[a] = s
        mv = jnp.maximum(mv, s)
    m = m_sc[...]
    m_new = jnp.maximum(m, jnp.max(mv, axis=-1, keepdims=True))
    alpha = jnp.exp2(m - m_new)
    lv = jnp.zeros((rows, 2 * LANES), F32)
    acc = alpha * acc_sc[...]
    for a in range(pps // 2):
        pr = jnp.exp2(s_sc[a] - m_new)
        lv = lv + pr
        acc = acc + jnp.dot(pr.astype(BF16), cbs[a], preferred_element_type=F32)
    l_sc[...] = alpha * l_sc[...] + jnp.sum(lv, axis=-1, keepdims=True)
    acc_sc[...] = acc
    m_sc[...] = m_new

    @pl.when(st == pl.num_programs(1) - 1)
    def _():
        kn = kn_ref[0]
        lanef = lax.broadcasted_iota(jnp.int32, (rows, LANES), 1).astype(F32)
        sn = jnp.where(lanef <= trow_ref[...], _nt_dot(q, kn), NEG)
        m1 = m_sc[...]
        m2 = jnp.maximum(m1, jnp.max(sn, axis=-1, keepdims=True))
        a2 = jnp.exp2(m1 - m2)
        pn = jnp.exp2(sn - m2)
        l = a2 * l_sc[...] + jnp.sum(pn, axis=-1, keepdims=True)
        acc2 = a2 * acc_sc[...] + jnp.dot(pn.astype(BF16), kn[:, :LANES], preferred_element_type=F32)
        o_ref[0] = (acc2 / l).astype(BF16)


def _mla_sample_call(page_table, q_rows, trow_col, k_new, cache_ckv, cache_krope):
    nb, n_pages = page_table.shape
    rows, w = q_rows.shape[1:]
    page = cache_ckv.shape[1]
    assert page == LANES and n_pages % PAGES_PER_STEP == 0
    pps = PAGES_PER_STEP

    def c_spec(p):
        return pl.BlockSpec((1, page, MLA_KV_LORA), lambda b, s, pt: (pt[b, s * pps + p], 0, 0))

    def r_spec(p):
        return pl.BlockSpec((1, MLA_ROPE, page), lambda b, s, pt: (pt[b, s * pps + p], 0, 0))

    per_seq = lambda shape: pl.BlockSpec((1,) + shape, lambda b, s, pt: (b, 0, 0))
    grid_spec = pltpu.PrefetchScalarGridSpec(
        num_scalar_prefetch=1,
        grid=(nb, n_pages // pps),
        in_specs=[per_seq((rows, w)), pl.BlockSpec((rows, 1), lambda b, s, pt: (0, 0)), per_seq((LANES, w))]
        + [c_spec(p) for p in range(pps)] + [r_spec(p) for p in range(pps)],
        out_specs=per_seq((rows, LANES)),
        scratch_shapes=[
            pltpu.VMEM((pps // 2, rows, 2 * LANES), F32),
            pltpu.VMEM((rows, 1), F32),
            pltpu.VMEM((rows, 1), F32),
            pltpu.VMEM((rows, LANES), F32),
        ],
    )
    return pl.pallas_call(
        functools.partial(_mla_sample_kernel, n_pages=n_pages),
        grid_spec=grid_spec,
        out_shape=jax.ShapeDtypeStruct((nb, rows, LANES), BF16),
        compiler_params=_cparams("arbitrary", "arbitrary"),
        name="mla_sample_attn",
    )(page_table, q_rows, trow_col, k_new, *([cache_ckv] * pps), *([cache_krope] * pps))


def _pad_last(w, width):
    return jnp.pad(w, [(0, 0)] * (w.ndim - 1) + [(0, width - w.shape[-1])])


def _moba_weights(w_qkv, w_o):
    d = w_qkv.shape[0]
    nq = MOBA_HEADS * MOBA_HEAD_DIM
    nk = MOBA_KV_HEADS * MOBA_HEAD_DIM
    wq = _pad_last(w_qkv[:, :nq].reshape(d, MOBA_HEADS, MOBA_HEAD_DIM), LANES).reshape(d, _QW)
    wk = w_qkv[:, nq:nq + nk]
    wv = w_qkv[:, nq + nk:]
    wkp = _pad_last(wk.reshape(d, MOBA_KV_HEADS, MOBA_HEAD_DIM), LANES).reshape(d, _KW)
    wvp = _pad_last(wv.reshape(d, MOBA_KV_HEADS, MOBA_HEAD_DIM), LANES).reshape(d, _KW)
    w_fused = jnp.concatenate([wq, wkp, wvp, wk, wv], axis=1).astype(BF16)
    wo = w_o.reshape(MOBA_HEADS, MOBA_HEAD_DIM, d)
    wo_prompt = jnp.pad(wo, ((0, 0), (0, LANES - MOBA_HEAD_DIM), (0, 0))).reshape(_QW, d).astype(BF16)
    kv_of = jnp.arange(MOBA_HEADS) // MOBA_GROUP
    onehot = (kv_of[:, None] == jnp.arange(MOBA_KV_HEADS)[None, :]).astype(F32)
    wo_sample = (onehot[:, :, None, None] * wo[:, None, :, :]).reshape(MOBA_HEADS * nk, d).astype(BF16)
    return w_fused, wo_prompt, wo_sample


def _rotate_half_cols(w):
    half = w.shape[-1] // 2
    return jnp.concatenate([-w[..., half:], w[..., :half]], axis=-1)


def _mla_weights(w_dq, w_uq, w_dkv, w_uk, w_uv, w_o):
    d = w_dq.shape[0]
    w_r = w_dkv[:, MLA_KV_LORA:]
    w1 = jnp.concatenate([w_dq, w_dkv[:, :MLA_KV_LORA], _pad_last(w_r, LANES), _pad_last(_rotate_half_cols(w_r), LANES)],
                         axis=1).astype(BF16)
    wq = w_uq.reshape(MLA_Q_LORA, MLA_HEADS, MLA_NOPE + MLA_ROPE)
    wq_nope = wq[:, :, :MLA_NOPE].transpose(1, 0, 2)
    wuk_t = w_uk.transpose(1, 2, 0)
    w_lat = _bmm(wq_nope, wuk_t).transpose(1, 0, 2).reshape(MLA_Q_LORA, MLA_HEADS * LANES).astype(BF16)
    wq_rope = wq[:, :, MLA_NOPE:]
    w_qr = _pad_last(wq_rope, LANES).reshape(MLA_Q_LORA, MLA_HEADS * LANES).astype(BF16)
    w_qrr = _pad_last(_rotate_half_cols(wq_rope), LANES).reshape(MLA_Q_LORA, MLA_HEADS * LANES).astype(BF16)
    wuv = w_uv.transpose(1, 0, 2)
    wo = w_o.reshape(MLA_HEADS, MLA_V, d)
    w_vo = _bmm(wuv, wo).reshape(MLA_HEADS * MLA_KV_LORA, d).astype(BF16)
    return w1, w_lat, w_qr, w_qrr, w_vo


def _rope_tables(pos):
    inv = ROPE_THETA ** (-jnp.arange(0, MLA_ROPE, 2, dtype=F32) / MLA_ROPE)
    ang = pos.astype(F32)[:, None] * inv[None, :]
    cos = jnp.cos(ang)
    sin = jnp.sin(ang)
    return _pad_last(jnp.concatenate([cos, cos], axis=-1), LANES), _pad_last(jnp.concatenate([sin, sin], axis=-1), LANES)


def kernel(x_prompt, x_sample, cache_moba_k, cache_moba_v, cache_mla_ckv, cache_mla_krope, page_table, c_prompt, c_sample, w_ada, b_ada, g_norm1, g_norm2, moba_w_qkv, moba_w_o, mla_w_dq, mla_g_q, mla_w_uq, mla_w_dkv, mla_g_kv, mla_w_uk, mla_w_uv, mla_w_o, moe_w_group, moe_w_expert, moe_w_gu, moe_w_down, g_final):
    bp, seq, d = x_prompt.shape
    assert bp == 1
    nb, tdec, _ = x_sample.shape
    n_pool, page = cache_moba_k.shape[:2]
    n_pages = page_table.shape[1]
    past = n_pages * page
    ms = nb * tdec
    xp = x_prompt.reshape(seq, d)
    xs = x_sample.reshape(ms, d)

    c_all = jnp.concatenate([c_sample, c_prompt, jnp.zeros((7, d), F32)], axis=0)
    ada = _ada_call(c_all, w_ada, b_ada).reshape(w_ada.shape[0], c_all.shape[0], N_ADA, d)

    def mods(layer):
        mp = [ada[layer, nb:nb + 1, i] for i in range(N_ADA)]
        msm = [jnp.repeat(ada[layer, :nb, i], tdec, axis=0) for i in range(N_ADA)]
        return mp, msm

    row = lambda v: v.reshape(1, -1)
    wg = [_pad_last(moe_w_group[l], LANES) for l in range(2)]
    we = [_pad_last(moe_w_expert[l], LANES) for l in range(2)]
    wgu = moe_w_gu.astype(BF16)
    wdn = moe_w_down.astype(BF16)
    gfin = row(g_final)

    def moe(x, layer, m, tm, final):
        return _moe_call(x, row(g_norm2[layer]), m[3], m[4], m[5], gfin, wg[layer], we[layer], wgu[layer], wdn[layer],
                         tm, final)

    mp, msm = mods(0)
    w_fused, wo_prompt, wo_sample = _moba_weights(moba_w_qkv, moba_w_o)
    g1 = row(g_norm1[0])
    slopes = jnp.exp2(-8.0 * jnp.arange(1, MOBA_HEADS + 1, dtype=F32) / MOBA_HEADS)

    q_aug, k_aug, v_pad, k_out, v_out = _moba_qkv_call(xp, g1, mp[0], mp[1], w_fused)
    hpc = 2
    slope_rows = jnp.repeat((slopes * LOG2E).reshape(MOBA_HEADS // hpc, hpc), MOBA_BLOCK, axis=1)
    al_tab = slope_rows[:, :, None] * jnp.arange(MOBA_BLOCK, dtype=F32)
    sl_tab = jnp.broadcast_to(slope_rows[:, :, None], slope_rows.shape + (LANES,))
    o_pad = _moba_attn_call(q_aug, k_aug, v_pad, al_tab, sl_tab)
    xp = _mm_res(o_pad, wo_prompt, xp, mp[2], 512)
    xp = moe(xp, 0, mp, 1024, False)
    moba_k_prompt = k_out.reshape(1, seq, MOBA_KV_HEADS, MOBA_HEAD_DIM)
    moba_v_prompt = v_out.reshape(1, seq, MOBA_KV_HEADS, MOBA_HEAD_DIM)

    nq = MOBA_HEADS * MOBA_HEAD_DIM
    nk = MOBA_KV_HEADS * MOBA_HEAD_DIM
    qkv_s = _norm_matmul(xs, g1, msm[0], msm[1], moba_w_qkv.astype(BF16), 256)
    q_s = qkv_s[:, :nq].reshape(nb, tdec, MOBA_HEADS, 1, MOBA_HEAD_DIM) * MOBA_HEAD_DIM ** -0.5
    k_s = qkv_s[:, nq:nq + nk]
    v_s = qkv_s[:, nq + nk:]
    kv_of = jnp.arange(MOBA_HEADS) // MOBA_GROUP
    head_kv = (kv_of[:, None] == jnp.arange(MOBA_KV_HEADS)[None, :]).astype(F32)
    rows = tdec * MOBA_HEADS
    qbd = (q_s * head_kv[None, None, :, :, None]).reshape(nb, rows, nk).astype(BF16)
    slope_col = jnp.tile(slopes, tdec).reshape(rows, 1)
    tpos_col = jnp.repeat(past + jnp.arange(tdec, dtype=F32), MOBA_HEADS).reshape(rows, 1)
    hk_mask = jnp.tile(jnp.repeat(head_kv, MOBA_HEAD_DIM, axis=1), (tdec, 1))
    pad_new = lambda a: jnp.pad(a.reshape(nb, tdec, -1), ((0, 0), (0, LANES - tdec), (0, 0))).astype(BF16)
    om = _moba_sample_call(page_table, qbd, slope_col, tpos_col, hk_mask, pad_new(k_s), pad_new(v_s),
                           cache_moba_k.transpose(0, 2, 3, 1).reshape(n_pool, nk, page),
                           cache_moba_v.transpose(0, 2, 3, 1).reshape(n_pool, nk, page))
    xs = _mm_res(om.reshape(ms, MOBA_HEADS * nk), wo_sample, xs, msm[2], 256)
    xs = moe(xs, 0, msm, 512, False)
    moba_k_sample = k_s.reshape(nb, tdec, MOBA_KV_HEADS, MOBA_HEAD_DIM)
    moba_v_sample = v_s.reshape(nb, tdec, MOBA_KV_HEADS, MOBA_HEAD_DIM)

    mp, msm = mods(1)
    w1, w_lat, w_qr, w_qrr, w_vo = _mla_weights(mla_w_dq, mla_w_uq, mla_w_dkv, mla_w_uk, mla_w_uv, mla_w_o)
    g1 = row(g_norm1[1])
    gq = row(mla_g_q)
    gkv = row(mla_g_kv)

    cos_p, sin_p = _rope_tables(jnp.arange(seq))
    q_full, ckv_p, kr_p, k_full = _mla_qkv_call(xp, g1, mp[0], mp[1], cos_p, sin_p, w1, gq, gkv, w_lat, w_qr, w_qrr, 256)
    o_lat = _mla_attn_call(q_full, k_full, 256, 512, 2)
    xp = _mm_res(o_lat, w_vo, xp, mp[2], 512)
    y_prompt = moe(xp, 1, mp, 1024, True).reshape(1, seq, d)

    cos_s, sin_s = _rope_tables(jnp.tile(past + jnp.arange(tdec), nb))
    q_fs, ckv_s, kr_s, k_fs = _mla_qkv_call(xs, g1, msm[0], msm[1], cos_s, sin_s, w1, gq, gkv, w_lat, w_qr, w_qrr, 256)
    q_rows = q_fs.reshape(MLA_HEADS, nb, tdec, 2 * LANES).transpose(1, 2, 0, 3).reshape(nb, tdec * MLA_HEADS, 2 * LANES)
    trow_col = jnp.repeat(jnp.arange(tdec, dtype=F32), MLA_HEADS).reshape(tdec * MLA_HEADS, 1)
    k_new = jnp.pad(k_fs.reshape(nb, tdec, 2 * LANES), ((0, 0), (0, LANES - tdec), (0, 0)))
    o_lat_s = _mla_sample_call(page_table, q_rows, trow_col, k_new, cache_mla_ckv, cache_mla_krope.transpose(0, 2, 1))
    xs = _mm_res(o_lat_s.reshape(ms, MLA_HEADS * LANES), w_vo, xs, msm[2], 256)
    y_sample = moe(xs, 1, msm, 512, True).reshape(nb, tdec, d)

    return (y_prompt, y_sample, moba_k_prompt, moba_v_prompt,
            ckv_p.reshape(1, seq, MLA_KV_LORA), kr_p.reshape(1, seq, MLA_ROPE),
            moba_k_sample, moba_v_sample,
            ckv_s.reshape(nb, tdec, MLA_KV_LORA), kr_s.reshape(nb, tdec, MLA_ROPE))
```

```python
import functools

import jax
import jax.numpy as jnp
from jax import lax
from jax.experimental import pallas as pl
from jax.experimental.pallas import tpu as pltpu

F32 = jnp.float32
BF16 = jnp.bfloat16
HIGHEST = lax.Precision.HIGHEST

D_MODEL = 1024
N_ADA = 6
NORM_EPS = 1e-6
MOBA_HEADS = 16
MOBA_KV_HEADS = 4
MOBA_GROUP = MOBA_HEADS // MOBA_KV_HEADS
MOBA_HEAD_DIM = 64
MOBA_BLOCK = 256
MOBA_TOPK = 3
MLA_HEADS = 16
MLA_Q_LORA = 256
MLA_KV_LORA = 128
MLA_NOPE = 64
MLA_ROPE = 32
MLA_V = 64
ROPE_THETA = 10000.0
N_GROUPS = 4
EXPERTS_PER_GROUP = 8
N_EXPERTS = N_GROUPS * EXPERTS_PER_GROUP
D_EXPERT = 256

LANES = 128
NEG = -1e30
LOG2E = 1.4426950408889634
VMEM_LIMIT = 56 * 1024 * 1024
MOBA_PAGES_PER_STEP = 32
PAGES_PER_STEP = 16

_NT = (((1,), (1,)), ((), ()))


def _nt_dot(a, b, **kw):
    return lax.dot_general(a, b, _NT, preferred_element_type=F32, **kw)


def _cparams(*sem):
    return pltpu.CompilerParams(dimension_semantics=sem, vmem_limit_bytes=VMEM_LIMIT)


def _rms_mod(x, g, shift, scale):
    y = x * lax.rsqrt(jnp.mean(x * x, axis=-1, keepdims=True) + NORM_EPS)
    return (y * g) * (1.0 + scale) + shift


def _top_mask_bias(vals, lanef, n_top, bias):
    for _ in range(n_top):
        m = jnp.max(vals, axis=-1, keepdims=True)
        idx = jnp.min(jnp.where(vals == m, lanef, 1e9), axis=-1, keepdims=True)
        idx = jnp.where(m > -jnp.inf, idx, -1.0)
        pick = lanef == idx
        bias = jnp.where(pick, 0.0, bias)
        vals = jnp.where(pick, -jnp.inf, vals)
    return bias


def _ada_kernel(c_ref, w_ref, b_ref, o_ref):
    c = c_ref[...]
    a = c * jax.nn.sigmoid(c)
    o_ref[0] = jnp.dot(a, w_ref[0], preferred_element_type=F32, precision=HIGHEST) + b_ref[0]


def _ada_call(c_all, w_ada, b_ada):
    depth, d, n = w_ada.shape
    rows = c_all.shape[0]
    tn = 512
    return pl.pallas_call(
        _ada_kernel,
        grid=(depth, n // tn),
        in_specs=[
            pl.BlockSpec((rows, d), lambda l, j: (0, 0)),
            pl.BlockSpec((1, d, tn), lambda l, j: (l, 0, j)),
            pl.BlockSpec((1, 1, tn), lambda l, j: (l, 0, j)),
        ],
        out_specs=pl.BlockSpec((1, rows, tn), lambda l, j: (l, 0, j)),
        out_shape=jax.ShapeDtypeStruct((depth, rows, n), F32),
        compiler_params=_cparams("arbitrary", "arbitrary"),
        name="ada_mod",
    )(c_all, w_ada, b_ada.reshape(depth, 1, n))


def _mod_spec(mod, tm):
    d = mod.shape[1]
    if mod.shape[0] == 1:
        return pl.BlockSpec((1, d), lambda i, *_: (0, 0))
    return pl.BlockSpec((tm, d), lambda i, *_: (i, 0))


def _norm_matmul_kernel(x_ref, g_ref, sh_ref, sc_ref, w_ref, o_ref):
    h = _rms_mod(x_ref[...], g_ref[...], sh_ref[...], sc_ref[...]).astype(BF16)
    o_ref[...] = jnp.dot(h, w_ref[...], preferred_element_type=F32)


def _norm_matmul(x, g, shift, scale, w, tm):
    m, d = x.shape
    tm = min(tm, m)
    n = w.shape[1]
    return pl.pallas_call(
        _norm_matmul_kernel,
        grid=(m // tm,),
        in_specs=[
            pl.BlockSpec((tm, d), lambda i: (i, 0)),
            pl.BlockSpec((1, d), lambda i: (0, 0)),
            _mod_spec(shift, tm),
            _mod_spec(scale, tm),
            pl.BlockSpec((d, n), lambda i: (0, 0)),
        ],
        out_specs=pl.BlockSpec((tm, n), lambda i: (i, 0)),
        out_shape=jax.ShapeDtypeStruct((m, n), F32),
        compiler_params=_cparams("arbitrary"),
        name="norm_matmul",
    )(x, g, shift, scale, w)


def _mm_res_kernel(a_ref, w_ref, x_ref, gate_ref, o_ref):
    y = jnp.dot(a_ref[...], w_ref[...], preferred_element_type=F32)
    o_ref[...] = x_ref[...] + gate_ref[...] * y


def _mm_res(a, w, x, gate, tm):
    m, k = a.shape
    tm = min(tm, m)
    n = w.shape[1]
    return pl.pallas_call(
        _mm_res_kernel,
        grid=(m // tm,),
        in_specs=[
            pl.BlockSpec((tm, k), lambda i: (i, 0)),
            pl.BlockSpec((k, n), lambda i: (0, 0)),
            pl.BlockSpec((tm, n), lambda i: (i, 0)),
            _mod_spec(gate, tm),
        ],
        out_specs=pl.BlockSpec((tm, n), lambda i: (i, 0)),
        out_shape=jax.ShapeDtypeStruct((m, n), F32),
        compiler_params=_cparams("arbitrary"),
        name="matmul_residual",
    )(a, w, x, gate)


def _bmm_kernel(a_ref, b_ref, o_ref):
    o_ref[0] = jnp.dot(a_ref[0], b_ref[0], preferred_element_type=F32, precision=HIGHEST)


def _bmm(a, b):
    n, m, k = a.shape
    p = b.shape[2]
    return pl.pallas_call(
        _bmm_kernel,
        grid=(n,),
        in_specs=[pl.BlockSpec((1, m, k), lambda i: (i, 0, 0)), pl.BlockSpec((1, k, p), lambda i: (i, 0, 0))],
        out_specs=pl.BlockSpec((1, m, p), lambda i: (i, 0, 0)),
        out_shape=jax.ShapeDtypeStruct((n, m, p), F32),
        compiler_params=_cparams("arbitrary"),
        name="fold_weights",
    )(a, b)


_QW = MOBA_HEADS * LANES
_KW = MOBA_KV_HEADS * LANES


def _moba_qkv_kernel(x_ref, g_ref, sh_ref, sc_ref, w_ref, q_ref, ka_ref, vp_ref, ko_ref, vo_ref, km_ref):
    i = pl.program_id(0)
    tm = x_ref.shape[0]
    half = LANES // 2

    @pl.when(i == 0)
    def _():
        km_ref[...] = jnp.zeros_like(km_ref)

    h = _rms_mod(x_ref[...], g_ref[...], sh_ref[...], sc_ref[...]).astype(BF16)
    y = jnp.dot(h, w_ref[...], preferred_element_type=F32)
    lane = lax.broadcasted_iota(jnp.int32, (tm, LANES), 1)
    lanef = lane.astype(F32)
    kvw = MOBA_KV_HEADS * MOBA_HEAD_DIM
    ko_ref[...] = y[:, _QW + 2 * _KW:_QW + 2 * _KW + kvw]
    vo_ref[...] = y[:, _QW + 2 * _KW + kvw:_QW + 2 * _KW + 2 * kvw]
    for k in range(MOBA_KV_HEADS):
        kp = y[:, _QW + k * LANES:_QW + (k + 1) * LANES]
        km_ref[k, pl.ds(half + i, 1), :] = jnp.mean(kp, axis=0, keepdims=True)
        ka_ref[:, k * LANES:(k + 1) * LANES] = jnp.where(lane == half + i, 1.0, kp).astype(BF16)
        vp = y[:, _QW + _KW + k * LANES:_QW + _KW + (k + 1) * LANES]
        vp_ref[:, k * LANES:(k + 1) * LANES] = jnp.where(lane == LANES - 1, 1.0, vp).astype(BF16)
    past = (lane >= half) & (lane < half + i)
    own_bias = jnp.where(lane == half + i, 0.0, NEG)
    for hd in range(MOBA_HEADS):
        qp = y[:, hd * LANES:(hd + 1) * LANES]
        gs = _nt_dot(qp, km_ref[hd // MOBA_GROUP], precision=HIGHEST)
        bias = _top_mask_bias(jnp.where(past, gs, -jnp.inf), lanef, MOBA_TOPK, own_bias)
        q_ref[hd] = jnp.where(lane < half, qp * (MOBA_HEAD_DIM ** -0.5 * LOG2E), bias).astype(BF16)


def _moba_qkv_call(x, g, shift, scale, w):
    t, d = x.shape
    tm = MOBA_BLOCK
    assert t // tm <= LANES // 2
    n = w.shape[1]
    kvw = MOBA_KV_HEADS * MOBA_HEAD_DIM
    return pl.pallas_call(
        _moba_qkv_kernel,
        grid=(t // tm,),
        in_specs=[
            pl.BlockSpec((tm, d), lambda i: (i, 0)),
            pl.BlockSpec((1, d), lambda i: (0, 0)),
            pl.BlockSpec((1, d), lambda i: (0, 0)),
            pl.BlockSpec((1, d), lambda i: (0, 0)),
            pl.BlockSpec((d, n), lambda i: (0, 0)),
        ],
        out_specs=[
            pl.BlockSpec((MOBA_HEADS, tm, LANES), lambda i: (0, i, 0)),
            pl.BlockSpec((tm, _KW), lambda i: (i, 0)),
            pl.BlockSpec((tm, _KW), lambda i: (i, 0)),
            pl.BlockSpec((tm, kvw), lambda i: (i, 0)),
            pl.BlockSpec((tm, kvw), lambda i: (i, 0)),
        ],
        out_shape=[
            jax.ShapeDtypeStruct((MOBA_HEADS, t, LANES), BF16),
            jax.ShapeDtypeStruct((t, _KW), BF16),
            jax.ShapeDtypeStruct((t, _KW), BF16),
            jax.ShapeDtypeStruct((t, kvw), F32),
            jax.ShapeDtypeStruct((t, kvw), F32),
        ],
        scratch_shapes=[pltpu.VMEM((MOBA_KV_HEADS, LANES, LANES), F32)],
        compiler_params=_cparams("arbitrary"),
        name="moba_qkv_gate",
    )(x, g, shift, scale, w)


def _moba_attn_kernel(q_ref, k_ref, v_ref, al_ref, sl_ref, o_ref, m_sc, acc_sc):
    i = pl.program_id(1)
    tq = q_ref.shape[1]
    blk = MOBA_BLOCK
    n_chains, rows, _ = m_sc.shape
    hpc = rows // tq
    m_sc[...] = jnp.full_like(m_sc, NEG)
    acc_sc[...] = jnp.zeros_like(acc_sc)

    def step(j, nblk, masked):
        tk = nblk * blk
        start = pl.multiple_of(j * blk, blk)
        kb = k_ref[pl.ds(start, tk), :]
        vb = v_ref[pl.ds(start, tk), :]
        dj = ((j - i) * blk).astype(F32)
        for u in range(n_chains):
            q = q_ref[u * hpc:(u + 1) * hpc].reshape(rows, LANES)
            s = _nt_dot(q, kb) + al_ref[u, :, :tk]
            if masked:
                row_pos = (nblk - 1) * blk + (lax.broadcasted_iota(jnp.int32, (rows, tk), 0) & (tq - 1))
                col = lax.broadcasted_iota(jnp.int32, (rows, tk), 1)
                s = jnp.where(col <= row_pos, s, NEG)
            c = sl_ref[u] * dj
            m_prev = m_sc[u]
            m_new = jnp.maximum(m_prev, jnp.max(s, axis=-1, keepdims=True) + c)
            shift = m_new - c
            p = jnp.exp2(s - jnp.concatenate([shift] * (tk // LANES), axis=1))
            acc_sc[u] = acc_sc[u] * jnp.exp2(m_prev - m_new) + jnp.dot(p.astype(BF16), vb, preferred_element_type=F32)
            m_sc[u] = m_new

    def past(jj, _):
        step(2 * jj, 2, False)
        return 0

    lax.fori_loop(0, i // 2, past, 0)

    @pl.when(i % 2 == 1)
    def _():
        step(i - 1, 2, True)

    @pl.when(i % 2 == 0)
    def _():
        step(i, 1, True)
    for u in range(n_chains):
        acc = acc_sc[u]
        o = acc / acc[:, LANES - 1:]
        for x in range(hpc):
            hd = u * hpc + x
            o_ref[:, hd * LANES:(hd + 1) * LANES] = o[x * tq:(x + 1) * tq].astype(BF16)


def _moba_attn_call(q_aug, k_aug, v_pad, al_tab, sl_tab):
    nh, t, _ = q_aug.shape
    tq = MOBA_BLOCK
    assert tq & (tq - 1) == 0
    n_chains, rows = al_tab.shape[0] // MOBA_KV_HEADS, al_tab.shape[1]
    gw = MOBA_GROUP * LANES
    return pl.pallas_call(
        _moba_attn_kernel,
        grid=(MOBA_KV_HEADS, t // tq),
        in_specs=[
            pl.BlockSpec((MOBA_GROUP, tq, LANES), lambda k, i: (k, i, 0)),
            pl.BlockSpec((t, LANES), lambda k, i: (0, k)),
            pl.BlockSpec((t, LANES), lambda k, i: (0, k)),
            pl.BlockSpec((n_chains, rows, 2 * MOBA_BLOCK), lambda k, i: (k, 0, 0)),
            pl.BlockSpec((n_chains, rows, LANES), lambda k, i: (k, 0, 0)),
        ],
        out_specs=pl.BlockSpec((tq, gw), lambda k, i: (i, k)),
        out_shape=jax.ShapeDtypeStruct((t, nh * LANES), BF16),
        scratch_shapes=[pltpu.VMEM((n_chains, rows, LANES), F32), pltpu.VMEM((n_chains, rows, LANES), F32)],
        compiler_params=_cparams("arbitrary", "arbitrary"),
        name="moba_attn",
    )(q_aug, k_aug, v_pad, al_tab, sl_tab)


def _moe_kernel(x_ref, g_ref, sh_ref, sc_ref, gt_ref, gf_ref, wg_ref, we_ref, wgu_ref, wdn_ref, o_ref,
                h_sc, gate_sc, acc_sc, *, final):
    e = pl.program_id(1)
    tm = x_ref.shape[0]
    lane = lax.broadcasted_iota(jnp.int32, (tm, LANES), 1)

    @pl.when(e == 0)
    def _():
        h = _rms_mod(x_ref[...], g_ref[...], sh_ref[...], sc_ref[...])
        h_sc[...] = h.astype(BF16)
        lanef = lane.astype(F32)
        gl = jnp.dot(h, wg_ref[...], preferred_element_type=F32, precision=HIGHEST)
        gl = jnp.where(lane < N_GROUPS, gl, -jnp.inf)
        gmax = jnp.max(gl, axis=-1, keepdims=True)
        g_sel = jnp.min(jnp.where(gl == gmax, lanef, 1e9), axis=-1, keepdims=True)
        g_p = 1.0 / jnp.sum(jnp.exp(gl - gmax), axis=-1, keepdims=True)
        el = jnp.dot(h, we_ref[...], preferred_element_type=F32, precision=HIGHEST)
        lo = g_sel * EXPERTS_PER_GROUP
        el = jnp.where((lanef >= lo) & (lanef < lo + EXPERTS_PER_GROUP), el, -jnp.inf)
        emax = jnp.max(el, axis=-1, keepdims=True)
        ex = jnp.exp(el - emax)
        prob = ex / jnp.sum(ex, axis=-1, keepdims=True)
        prob = jnp.where(el > -jnp.inf, prob, -jnp.inf)
        p1 = jnp.max(prob, axis=-1, keepdims=True)
        i1 = jnp.min(jnp.where(prob == p1, lanef, 1e9), axis=-1, keepdims=True)
        prob2 = jnp.where(lanef == i1, -jnp.inf, prob)
        p2 = jnp.max(prob2, axis=-1, keepdims=True)
        i2 = jnp.min(jnp.where(prob2 == p2, lanef, 1e9), axis=-1, keepdims=True)
        den = p1 + p2
        gate_sc[...] = jnp.where(lanef == i1, p1 / den * g_p, jnp.where(lanef == i2, p2 / den * g_p, 0.0))
        acc_sc[...] = jnp.zeros_like(acc_sc)

    gcol = jnp.sum(jnp.where(lane == e, gate_sc[...], 0.0), axis=-1, keepdims=True)
    gu = jnp.dot(h_sc[...], wgu_ref[0], preferred_element_type=F32)
    gg = gu[:, :D_EXPERT]
    a = (gg * jax.nn.sigmoid(gg)) * gu[:, D_EXPERT:] * gcol
    acc_sc[...] += jnp.dot(a.astype(BF16), wdn_ref[0], preferred_element_type=F32)

    @pl.when(e == pl.num_programs(1) - 1)
    def _():
        y = x_ref[...] + gt_ref[...] * acc_sc[...]
        if final:
            y = y * lax.rsqrt(jnp.mean(y * y, axis=-1, keepdims=True) + NORM_EPS) * gf_ref[...]
        o_ref[...] = y


def _moe_call(x, g, shift, scale, gate, g_final, w_group, w_expert, w_gu, w_down, tm, final):
    m, d = x.shape
    tm = min(tm, m)
    ne, _, f2 = w_gu.shape
    fd = w_down.shape[1]
    return pl.pallas_call(
        functools.partial(_moe_kernel, final=final),
        grid=(m // tm, ne),
        in_specs=[
            pl.BlockSpec((tm, d), lambda i, e: (i, 0)),
            pl.BlockSpec((1, d), lambda i, e: (0, 0)),
            _mod_spec(shift, tm),
            _mod_spec(scale, tm),
            _mod_spec(gate, tm),
            pl.BlockSpec((1, d), lambda i, e: (0, 0)),
            pl.BlockSpec((d, LANES), lambda i, e: (0, 0)),
            pl.BlockSpec((d, LANES), lambda i, e: (0, 0)),
            pl.BlockSpec((1, d, f2), lambda i, e: (e, 0, 0)),
            pl.BlockSpec((1, fd, d), lambda i, e: (e, 0, 0)),
        ],
        out_specs=pl.BlockSpec((tm, d), lambda i, e: (i, 0)),
        out_shape=jax.ShapeDtypeStruct((m, d), F32),
        scratch_shapes=[pltpu.VMEM((tm, d), BF16), pltpu.VMEM((tm, LANES), F32), pltpu.VMEM((tm, d), F32)],
        compiler_params=_cparams("arbitrary", "arbitrary"),
        name="hier_moe",
    )(x, g, shift, scale, gate, g_final, w_group, w_expert, w_gu, w_down)


_MLA_W1 = MLA_Q_LORA + MLA_KV_LORA + 2 * LANES


def _mla_qkv_kernel(x_ref, g_ref, sh_ref, sc_ref, cos_ref, sin_ref, w1_ref, gq_ref, gkv_ref, wl_ref, wr_ref, wrr_ref,
                    q_ref, ckv_ref, kr_ref, kf_ref):
    h = _rms_mod(x_ref[...], g_ref[...], sh_ref[...], sc_ref[...]).astype(BF16)
    a = jnp.dot(h, w1_ref[...], preferred_element_type=F32)
    cos = cos_ref[...]
    sin = sin_ref[...]
    cq = a[:, :MLA_Q_LORA]
    cqn = (cq * lax.rsqrt(jnp.mean(cq * cq, axis=-1, keepdims=True) + NORM_EPS) * gq_ref[...]).astype(BF16)
    kvc = a[:, MLA_Q_LORA:MLA_Q_LORA + MLA_KV_LORA]
    ckv = kvc * lax.rsqrt(jnp.mean(kvc * kvc, axis=-1, keepdims=True) + NORM_EPS) * gkv_ref[...]
    o = MLA_Q_LORA + MLA_KV_LORA
    kr = a[:, o:o + LANES] * cos + a[:, o + LANES:o + 2 * LANES] * sin
    ckv_ref[...] = ckv
    kr_ref[...] = kr[:, :MLA_ROPE]
    kf_ref[:, :LANES] = ckv.astype(BF16)
    lane = lax.broadcasted_iota(jnp.int32, kr.shape, 1)
    kf_ref[:, LANES:] = jnp.where(lane == LANES - 1, 1.0, kr).astype(BF16)
    scale = (MLA_NOPE + MLA_ROPE) ** -0.5 * LOG2E
    ql = jnp.dot(cqn, wl_ref[...], preferred_element_type=F32)
    qr = jnp.dot(cqn, wr_ref[...], preferred_element_type=F32)
    qrr = jnp.dot(cqn, wrr_ref[...], preferred_element_type=F32)
    for hd in range(MLA_HEADS):
        sl = slice(hd * LANES, (hd + 1) * LANES)
        q_ref[hd, :, :LANES] = (ql[:, sl] * scale).astype(BF16)
        q_ref[hd, :, LANES:] = ((qr[:, sl] * cos + qrr[:, sl] * sin) * scale).astype(BF16)


def _mla_qkv_call(x, g, shift, scale, cos, sin, w1, gq, gkv, wl, wr, wrr, tm):
    m, d = x.shape
    tm = min(tm, m)
    hw = MLA_HEADS * LANES
    full = lambda shape: pl.BlockSpec(shape, lambda i: (0,) * len(shape))
    return pl.pallas_call(
        _mla_qkv_kernel,
        grid=(m // tm,),
        in_specs=[
            pl.BlockSpec((tm, d), lambda i: (i, 0)),
            full((1, d)),
            _mod_spec(shift, tm),
            _mod_spec(scale, tm),
            pl.BlockSpec((tm, LANES), lambda i: (i, 0)),
            pl.BlockSpec((tm, LANES), lambda i: (i, 0)),
            full((d, _MLA_W1)),
            full((1, MLA_Q_LORA)),
            full((1, MLA_KV_LORA)),
            full((MLA_Q_LORA, hw)),
            full((MLA_Q_LORA, hw)),
            full((MLA_Q_LORA, hw)),
        ],
        out_specs=[
            pl.BlockSpec((MLA_HEADS, tm, 2 * LANES), lambda i: (0, i, 0)),
            pl.BlockSpec((tm, MLA_KV_LORA), lambda i: (i, 0)),
            pl.BlockSpec((tm, MLA_ROPE), lambda i: (i, 0)),
            pl.BlockSpec((tm, 2 * LANES), lambda i: (i, 0)),
        ],
        out_shape=[
            jax.ShapeDtypeStruct((MLA_HEADS, m, 2 * LANES), BF16),
            jax.ShapeDtypeStruct((m, MLA_KV_LORA), F32),
            jax.ShapeDtypeStruct((m, MLA_ROPE), F32),
            jax.ShapeDtypeStruct((m, 2 * LANES), BF16),
        ],
        compiler_params=_cparams("arbitrary"),
        name="mla_qkv",
    )(x, g, shift, scale, cos, sin, w1, gq, gkv, wl, wr, wrr)


def _mla_attn_kernel(q_ref, k_ref, o_ref, m_sc, acc_sc, *, tk, hpc, cpb):
    i = pl.program_id(0)
    nh, tq, w = q_ref.shape
    rows = hpc * tq
    n_chains = nh // hpc
    n_past = (i * tq) // tk
    m_sc[...] = jnp.full_like(m_sc, NEG)
    acc_sc[...] = jnp.zeros_like(acc_sc)

    def tile(j, masked):
        start = pl.multiple_of(j * tk, tk)
        kb = k_ref[pl.ds(start, tk), :]

        def chain_pair(cp, _):
            for u in range(cpb):
                c = cp * cpb + u
                q = q_ref[pl.ds(c * hpc, hpc)].reshape(rows, w)
                s = _nt_dot(q, kb)
                if masked:
                    row_pos = i * tq + (lax.broadcasted_iota(jnp.int32, (rows, tk), 0) & (tq - 1))
                    col_pos = start + lax.broadcasted_iota(jnp.int32, (rows, tk), 1)
                    s = jnp.where(col_pos <= row_pos, s, NEG)
                m_prev = m_sc[c]
                m_new = jnp.maximum(m_prev, jnp.max(s, axis=-1, keepdims=True))
                p = jnp.exp2(s - jnp.concatenate([m_new] * (tk // LANES), axis=1))
                alpha = jnp.exp2(m_prev - m_new)
                acc_sc[c] = (acc_sc[c] * jnp.concatenate([alpha] * (w // LANES), axis=1)
                             + jnp.dot(p.astype(BF16), kb, preferred_element_type=F32))
                m_sc[c] = m_new
            return 0

        lax.fori_loop(0, n_chains // cpb, chain_pair, 0)

    def past(j, _):
        tile(j, False)
        return 0

    lax.fori_loop(0, n_past, past, 0)
    tile(n_past, True)
    for c in range(n_chains):
        acc = acc_sc[c]
        o = acc[:, :LANES] / acc[:, w - 1:]
        for x in range(hpc):
            hd = c * hpc + x
            o_ref[:, hd * LANES:(hd + 1) * LANES] = o[x * tq:(x + 1) * tq].astype(BF16)


def _mla_attn_call(q_full, k_full, tq, tk, hpc, cpb):
    nh, t, w = q_full.shape
    assert tq & (tq - 1) == 0 and tk % tq == 0 and t % tk == 0 and nh % (cpb * hpc) == 0
    n_chains, rows = nh // hpc, hpc * tq
    return pl.pallas_call(
        functools.partial(_mla_attn_kernel, tk=tk, hpc=hpc, cpb=cpb),
        grid=(t // tq,),
        in_specs=[
            pl.BlockSpec((nh, tq, w), lambda i: (0, i, 0)),
            pl.BlockSpec((t, w), lambda i: (0, 0)),
        ],
        out_specs=pl.BlockSpec((tq, nh * LANES), lambda i: (i, 0)),
        out_shape=jax.ShapeDtypeStruct((t, nh * LANES), BF16),
        scratch_shapes=[pltpu.VMEM((n_chains, rows, LANES), F32), pltpu.VMEM((n_chains, rows, w), F32)],
        compiler_params=_cparams("arbitrary"),
        name="mla_attn",
    )(q_full, k_full)


def _moba_sample_kernel(pt_ref, q_ref, slope_ref, tpos_ref, hk_ref, kn_ref, vn_ref, *rest, n_pages, past_len):
    pps = MOBA_PAGES_PER_STEP
    kp = rest[:pps]
    vp = rest[pps:2 * pps]
    o_ref = rest[2 * pps]
    s_sc, p_sc, g_sc, b_sc, pn_sc, l_sc, acc_sc = rest[2 * pps + 1:]
    st = pl.program_id(1)
    n_chunks = n_pages // pps
    n_blocks = n_pages // 2
    q = q_ref[0]
    rows = q.shape[0]
    lane = lax.broadcasted_iota(jnp.int32, (rows, LANES), 1)
    lanef = lane.astype(F32)

    @pl.when(st < n_chunks)
    def _():
        for a in range(pps // 2):
            kb = jnp.concatenate([kp[2 * a][0].astype(BF16), kp[2 * a + 1][0].astype(BF16)], axis=1)
            s2 = jnp.dot(q, kb, preferred_element_type=F32)
            s_sc[st * pps + 2 * a] = s2[:, :LANES]
            s_sc[st * pps + 2 * a + 1] = s2[:, LANES:]

    @pl.when(st == n_chunks - 1)
    def _():
        for b in range(n_blocks):
            r = jnp.sum(s_sc[2 * b] + s_sc[2 * b + 1], axis=-1, keepdims=True)
            g_sc[b] = jnp.broadcast_to(r, (rows, LANES))
            b_sc[b] = jnp.full((rows, LANES), NEG, F32)
        for _ in range(min(MOBA_TOPK, n_blocks)):
            best = g_sc[0]
            for b in range(1, n_blocks):
                best = jnp.maximum(best, g_sc[b])
            idx = jnp.full((rows, LANES), float(n_blocks), F32)
            for b in reversed(range(n_blocks)):
                idx = jnp.where(g_sc[b] == best, float(b), idx)
            for b in range(n_blocks):
                hit = idx == float(b)
                b_sc[b] = jnp.where(hit, 0.0, b_sc[b])
                g_sc[b] = jnp.where(hit, -jnp.inf, g_sc[b])
        slope = jnp.broadcast_to(slope_ref[...], (rows, LANES))
        tpos = jnp.broadcast_to(tpos_ref[...], (rows, LANES))
        al_lane = slope * lanef
        sn = _nt_dot(q, kn_ref[0]) - slope * (tpos - (past_len + lanef))
        sn = jnp.where(past_len + lanef <= tpos, sn, NEG)
        mv = sn
        for b in range(n_blocks):
            for u in range(2):
                pg = 2 * b + u
                sc = s_sc[pg] + al_lane + (b_sc[b] + slope * (float(pg * LANES) - tpos))
                s_sc[pg] = sc
                mv = jnp.maximum(mv, sc)
        m = jnp.broadcast_to(jnp.max(mv, axis=-1, keepdims=True), (rows, LANES))
        pn = jnp.exp(sn - m)
        lv = pn
        for pg in range(n_pages):
            p = jnp.exp(s_sc[pg] - m)
            p_sc[pg] = p.astype(BF16)
            lv = lv + p
        l_sc[...] = jnp.sum(lv, axis=-1, keepdims=True)
        pn_sc[...] = pn.astype(BF16)
        acc_sc[...] = jnp.zeros_like(acc_sc)

    @pl.when(st >= n_chunks)
    def _():
        c = st - n_chunks
        acc = acc_sc[...]
        for a in range(pps // 2):
            pg = c * pps + 2 * a
            p2 = jnp.concatenate([p_sc[pg], p_sc[pg + 1]], axis=1)
            vt = jnp.concatenate([vp[2 * a][0].astype(BF16), vp[2 * a + 1][0].astype(BF16)], axis=1)
            acc += _nt_dot(p2, vt)
        acc_sc[...] = acc

    @pl.when(st == 2 * n_chunks - 1)
    def _():
        acc = acc_sc[...] + jnp.dot(pn_sc[...], vn_ref[0], preferred_element_type=F32)
        o_ref[0] = (acc / l_sc[...] * hk_ref[...]).astype(BF16)


def _moba_sample_call(page_table, qbd, slope_col, tpos_col, hk_mask, k_new, v_new, cache_k, cache_v):
    nb, n_pages = page_table.shape
    rows, w = qbd.shape[1:]
    page = cache_k.shape[2]
    pps = MOBA_PAGES_PER_STEP
    assert page == LANES and n_pages % pps == 0 and pps % 2 == 0
    n_chunks = n_pages // pps

    def k_spec(p):
        return pl.BlockSpec((1, w, page), lambda b, s, pt: (pt[b, jnp.minimum(s, n_chunks - 1) * pps + p], 0, 0))

    def v_spec(p):
        return pl.BlockSpec((1, w, page), lambda b, s, pt: (pt[b, jnp.maximum(s - n_chunks, 0) * pps + p], 0, 0))

    per_seq = lambda shape: pl.BlockSpec((1,) + shape, lambda b, s, pt: (b, 0, 0))
    const = lambda shape: pl.BlockSpec(shape, lambda b, s, pt: (0, 0))
    grid_spec = pltpu.PrefetchScalarGridSpec(
        num_scalar_prefetch=1,
        grid=(nb, 2 * n_chunks),
        in_specs=[per_seq((rows, w)), const((rows, 1)), const((rows, 1)), const((rows, w)),
                  per_seq((LANES, w)), per_seq((LANES, w))]
        + [k_spec(p) for p in range(pps)] + [v_spec(p) for p in range(pps)],
        out_specs=per_seq((rows, w)),
        scratch_shapes=[
            pltpu.VMEM((n_pages, rows, LANES), F32),
            pltpu.VMEM((n_pages, rows, LANES), BF16),
            pltpu.VMEM((n_pages // 2, rows, LANES), F32),
            pltpu.VMEM((n_pages // 2, rows, LANES), F32),
            pltpu.VMEM((rows, LANES), BF16),
            pltpu.VMEM((rows, 1), F32),
            pltpu.VMEM((rows, w), F32),
        ],
    )
    return pl.pallas_call(
        functools.partial(_moba_sample_kernel, n_pages=n_pages, past_len=n_pages * page),
        grid_spec=grid_spec,
        out_shape=jax.ShapeDtypeStruct((nb, rows, w), BF16),
        compiler_params=_cparams("arbitrary", "arbitrary"),
        name="moba_sample_attn",
    )(page_table, qbd, slope_col, tpos_col, hk_mask, k_new, v_new, *([cache_k] * pps), *([cache_v] * pps))


def _mla_sample_kernel(pt_ref, q_ref, trow_ref, kn_ref, *rest, n_pages):
    pps = PAGES_PER_STEP
    cp = rest[:pps]
    rp = rest[pps:2 * pps]
    o_ref = rest[2 * pps]
    s_sc, m_sc, l_sc, acc_sc = rest[2 * pps + 1:]
    st = pl.program_id(1)
    q = q_ref[0]
    ql = q[:, :LANES]
    qr = q[:, LANES:LANES + MLA_ROPE]
    rows = q.shape[0]

    @pl.when(st == 0)
    def _():
        m_sc[...] = jnp.full_like(m_sc, NEG)
        l_sc[...] = jnp.zeros_like(l_sc)
        acc_sc[...] = jnp.zeros_like(acc_sc)

    mv = jnp.full((rows, 2 * LANES), NEG, F32)
    cbs = []
    for a in range(pps // 2):
        cb = jnp.concatenate([cp[2 * a][0].astype(BF16), cp[2 * a + 1][0].astype(BF16)], axis=0)
        rb = jnp.concatenate([rp[2 * a][0].astype(BF16), rp[2 * a + 1][0].astype(BF16)], axis=1)
        cbs.append(cb)
        s = _nt_dot(ql, cb) + jnp.dot(qr, rb, preferred_element_type=F32)
        s_sc[a] = s
        mv = jnp.maximum(mv, s)
    m = m_sc[...]
    m_new = jnp.maximum(m, jnp.max(mv, axis=-1, keepdims=True))
    alpha = jnp.exp2(m - m_new)
    lv = jnp.zeros((rows, 2 * LANES), F32)
    acc = alpha * acc_sc[...]
    for a in range(pps // 2):
        pr = jnp.exp2(s_sc[a] - m_new)
        lv = lv + pr
        acc = acc + jnp.dot(pr.astype(BF16), cbs[a], preferred_element_type=F32)
    l_sc[...] = alpha * l_sc[...] + jnp.sum(lv, axis=-1, keepdims=True)
    acc_sc[...] = acc
    m_sc[...] = m_new

    @pl.when(st == pl.num_programs(1) - 1)
    def _():
        kn = kn_ref[0]
        lanef = lax.broadcasted_iota(jnp.int32, (rows, LANES), 1).astype(F32)
        sn = jnp.where(lanef <= trow_ref[...], _nt_dot(q, kn), NEG)
        m1 = m_sc[...]
        m2 = jnp.maximum(m1, jnp.max(sn, axis=-1, keepdims=True))
        a2 = jnp.exp2(m1 - m2)
        pn = jnp.exp2(sn - m2)
        l = a2 * l_sc[...] + jnp.sum(pn, axis=-1, keepdims=True)
        acc2 = a2 * acc_sc[...] + jnp.dot(pn.astype(BF16), kn[:, :LANES], preferred_element_type=F32)
        o_ref[0] = (acc2 / l).astype(BF16)


def _mla_sample_call(page_table, q_rows, trow_col, k_new, cache_ckv, cache_krope):
    nb, n_pages = page_table.shape
    rows, w = q_rows.shape[1:]
    page = cache_ckv.shape[1]
    assert page == LANES and n_pages % PAGES_PER_STEP == 0
    pps = PAGES_PER_STEP

    def c_spec(p):
        return pl.BlockSpec((1, page, MLA_KV_LORA), lambda b, s, pt: (pt[b, s * pps + p], 0, 0))

    def r_spec(p):
        return pl.BlockSpec((1, MLA_ROPE, page), lambda b, s, pt: (pt[b, s * pps + p], 0, 0))

    per_seq = lambda shape: pl.BlockSpec((1,) + shape, lambda b, s, pt: (b, 0, 0))
    grid_spec = pltpu.PrefetchScalarGridSpec(
        num_scalar_prefetch=1,
        grid=(nb, n_pages // pps),
        in_specs=[per_seq((rows, w)), pl.BlockSpec((rows, 1), lambda b, s, pt: (0, 0)), per_seq((LANES, w))]
        + [c_spec(p) for p in range(pps)] + [r_spec(p) for p in range(pps)],
        out_specs=per_seq((rows, LANES)),
        scratch_shapes=[
            pltpu.VMEM((pps // 2, rows, 2 * LANES), F32),
            pltpu.VMEM((rows, 1), F32),
            pltpu.VMEM((rows, 1), F32),
            pltpu.VMEM((rows, LANES), F32),
        ],
    )
    return pl.pallas_call(
        functools.partial(_mla_sample_kernel, n_pages=n_pages),
        grid_spec=grid_spec,
        out_shape=jax.ShapeDtypeStruct((nb, rows, LANES), BF16),
        compiler_params=_cparams("arbitrary", "arbitrary"),
        name="mla_sample_attn",
    )(page_table, q_rows, trow_col, k_new, *([cache_ckv] * pps), *([cache_krope] * pps))


def _pad_last(w, width):
    return jnp.pad(w, [(0, 0)] * (w.ndim - 1) + [(0, width - w.shape[-1])])


def _moba_weights(w_qkv, w_o):
    d = w_qkv.shape[0]
    nq = MOBA_HEADS * MOBA_HEAD_DIM
    nk = MOBA_KV_HEADS * MOBA_HEAD_DIM
    wq = _pad_last(w_qkv[:, :nq].reshape(d, MOBA_HEADS, MOBA_HEAD_DIM), LANES).reshape(d, _QW)
    wk = w_qkv[:, nq:nq + nk]
    wv = w_qkv[:, nq + nk:]
    wkp = _pad_last(wk.reshape(d, MOBA_KV_HEADS, MOBA_HEAD_DIM), LANES).reshape(d, _KW)
    wvp = _pad_last(wv.reshape(d, MOBA_KV_HEADS, MOBA_HEAD_DIM), LANES).reshape(d, _KW)
    w_fused = jnp.concatenate([wq, wkp, wvp, wk, wv], axis=1).astype(BF16)
    wo = w_o.reshape(MOBA_HEADS, MOBA_HEAD_DIM, d)
    wo_prompt = jnp.pad(wo, ((0, 0), (0, LANES - MOBA_HEAD_DIM), (0, 0))).reshape(_QW, d).astype(BF16)
    kv_of = jnp.arange(MOBA_HEADS) // MOBA_GROUP
    onehot = (kv_of[:, None] == jnp.arange(MOBA_KV_HEADS)[None, :]).astype(F32)
    wo_sample = (onehot[:, :, None, None] * wo[:, None, :, :]).reshape(MOBA_HEADS * nk, d).astype(BF16)
    return w_fused, wo_prompt, wo_sample


def _rotate_half_cols(w):
    half = w.shape[-1] // 2
    return jnp.concatenate([-w[..., half:], w[..., :half]], axis=-1)


def _mla_weights(w_dq, w_uq, w_dkv, w_uk, w_uv, w_o):
    d = w_dq.shape[0]
    w_r = w_dkv[:, MLA_KV_LORA:]
    w1 = jnp.concatenate([w_dq, w_dkv[:, :MLA_KV_LORA], _pad_last(w_r, LANES), _pad_last(_rotate_half_cols(w_r), LANES)],
                         axis=1).astype(BF16)
    wq = w_uq.reshape(MLA_Q_LORA, MLA_HEADS, MLA_NOPE + MLA_ROPE)
    wq_nope = wq[:, :, :MLA_NOPE].transpose(1, 0, 2)
    wuk_t = w_uk.transpose(1, 2, 0)
    w_lat = _bmm(wq_nope, wuk_t).transpose(1, 0, 2).reshape(MLA_Q_LORA, MLA_HEADS * LANES).astype(BF16)
    wq_rope = wq[:, :, MLA_NOPE:]
    w_qr = _pad_last(wq_rope, LANES).reshape(MLA_Q_LORA, MLA_HEADS * LANES).astype(BF16)
    w_qrr = _pad_last(_rotate_half_cols(wq_rope), LANES).reshape(MLA_Q_LORA, MLA_HEADS * LANES).astype(BF16)
    wuv = w_uv.transpose(1, 0, 2)
    wo = w_o.reshape(MLA_HEADS, MLA_V, d)
    w_vo = _bmm(wuv, wo).reshape(MLA_HEADS * MLA_KV_LORA, d).astype(BF16)
    return w1, w_lat, w_qr, w_qrr, w_vo


def _rope_tables(pos):
    inv = ROPE_THETA ** (-jnp.arange(0, MLA_ROPE, 2, dtype=F32) / MLA_ROPE)
    ang = pos.astype(F32)[:, None] * inv[None, :]
    cos = jnp.cos(ang)
    sin = jnp.sin(ang)
    return _pad_last(jnp.concatenate([cos, cos], axis=-1), LANES), _pad_last(jnp.concatenate([sin, sin], axis=-1), LANES)


def kernel(x_prompt, x_sample, cache_moba_k, cache_moba_v, cache_mla_ckv, cache_mla_krope, page_table, c_prompt, c_sample, w_ada, b_ada, g_norm1, g_norm2, moba_w_qkv, moba_w_o, mla_w_dq, mla_g_q, mla_w_uq, mla_w_dkv, mla_g_kv, mla_w_uk, mla_w_uv, mla_w_o, moe_w_group, moe_w_expert, moe_w_gu, moe_w_down, g_final):
    bp, seq, d = x_prompt.shape
    assert bp == 1
    nb, tdec, _ = x_sample.shape
    n_pool, page = cache_moba_k.shape[:2]
    n_pages = page_table.shape[1]
    past = n_pages * page
    ms = nb * tdec
    xp = x_prompt.reshape(seq, d)
    xs = x_sample.reshape(ms, d)

    c_all = jnp.concatenate([c_sample, c_prompt, jnp.zeros((7, d), F32)], axis=0)
    ada = _ada_call(c_all, w_ada, b_ada).reshape(w_ada.shape[0], c_all.shape[0], N_ADA, d)

    def mods(layer):
        mp = [ada[layer, nb:nb + 1, i] for i in range(N_ADA)]
        msm = [jnp.repeat(ada[layer, :nb, i], tdec, axis=0) for i in range(N_ADA)]
        return mp, msm

    row = lambda v: v.reshape(1, -1)
    wg = [_pad_last(moe_w_group[l], LANES) for l in range(2)]
    we = [_pad_last(moe_w_expert[l], LANES) for l in range(2)]
    wgu = moe_w_gu.astype(BF16)
    wdn = moe_w_down.astype(BF16)
    gfin = row(g_final)

    def moe(x, layer, m, tm, final):
        return _moe_call(x, row(g_norm2[layer]), m[3], m[4], m[5], gfin, wg[layer], we[layer], wgu[layer], wdn[layer],
                         tm, final)

    mp, msm = mods(0)
    w_fused, wo_prompt, wo_sample = _moba_weights(moba_w_qkv, moba_w_o)
    g1 = row(g_norm1[0])
    slopes = jnp.exp2(-8.0 * jnp.arange(1, MOBA_HEADS + 1, dtype=F32) / MOBA_HEADS)

    q_aug, k_aug, v_pad, k_out, v_out = _moba_qkv_call(xp, g1, mp[0], mp[1], w_fused)
    hpc = 2
    slope_rows =jnp.repeat((slopes * LOG2E).reshape(MOBA_HEADS // hpc, hpc), MOBA_BLOCK, axis=1)
    al_tab = slope_rows[:, :, None] * jnp.arange(2 * MOBA_BLOCK, dtype=F32)
    sl_tab = jnp.broadcast_to(slope_rows[:, :, None], slope_rows.shape + (LANES,))
    o_pad = _moba_attn_call(q_aug, k_aug, v_pad, al_tab, sl_tab)
    xp = _mm_res(o_pad, wo_prompt, xp, mp[2], 512)
    xp = moe(xp, 0, mp, 1024, False)
    moba_k_prompt = k_out.reshape(1, seq, MOBA_KV_HEADS, MOBA_HEAD_DIM)
    moba_v_prompt = v_out.reshape(1, seq, MOBA_KV_HEADS, MOBA_HEAD_DIM)

    nq = MOBA_HEADS * MOBA_HEAD_DIM
    nk = MOBA_KV_HEADS * MOBA_HEAD_DIM
    qkv_s = _norm_matmul(xs, g1, msm[0], msm[1], moba_w_qkv.astype(BF16), 256)
    q_s = qkv_s[:, :nq].reshape(nb, tdec, MOBA_HEADS, 1, MOBA_HEAD_DIM) * MOBA_HEAD_DIM ** -0.5
    k_s = qkv_s[:, nq:nq + nk]
    v_s = qkv_s[:, nq + nk:]
    kv_of = jnp.arange(MOBA_HEADS) // MOBA_GROUP
    head_kv = (kv_of[:, None] == jnp.arange(MOBA_KV_HEADS)[None, :]).astype(F32)
    rows = tdec * MOBA_HEADS
    qbd = (q_s * head_kv[None, None, :, :, None]).reshape(nb, rows, nk).astype(BF16)
    slope_col = jnp.tile(slopes, tdec).reshape(rows, 1)
    tpos_col = jnp.repeat(past + jnp.arange(tdec, dtype=F32), MOBA_HEADS).reshape(rows, 1)
    hk_mask = jnp.tile(jnp.repeat(head_kv, MOBA_HEAD_DIM, axis=1), (tdec, 1))
    pad_new = lambda a: jnp.pad(a.reshape(nb, tdec, -1), ((0, 0), (0, LANES - tdec), (0, 0))).astype(BF16)
    om = _moba_sample_call(page_table, qbd, slope_col, tpos_col, hk_mask, pad_new(k_s), pad_new(v_s),
                           cache_moba_k.transpose(0, 2, 3, 1).reshape(n_pool, nk, page),
                           cache_moba_v.transpose(0, 2, 3, 1).reshape(n_pool, nk, page))
    xs = _mm_res(om.reshape(ms, MOBA_HEADS * nk), wo_sample, xs, msm[2], 256)
    xs = moe(xs, 0, msm, 512, False)
    moba_k_sample = k_s.reshape(nb, tdec, MOBA_KV_HEADS, MOBA_HEAD_DIM)
    moba_v_sample = v_s.reshape(nb, tdec, MOBA_KV_HEADS, MOBA_HEAD_DIM)

    mp, msm = mods(1)
    w1, w_lat, w_qr, w_qrr, w_vo = _mla_weights(mla_w_dq, mla_w_uq, mla_w_dkv, mla_w_uk, mla_w_uv, mla_w_o)
    g1 = row(g_norm1[1])
    gq = row(mla_g_q)
    gkv = row(mla_g_kv)

    cos_p, sin_p = _rope_tables(jnp.arange(seq))
    q_full, ckv_p, kr_p, k_full = _mla_qkv_call(xp, g1, mp[0], mp[1], cos_p, sin_p, w1, gq, gkv, w_lat, w_qr, w_qrr, 256)
    o_lat = _mla_attn_call(q_full, k_full, 256, 512, 2, 8)
    xp = _mm_res(o_lat, w_vo, xp, mp[2], 512)
    y_prompt = moe(xp, 1, mp, 1024, True).reshape(1, seq, d)

    cos_s, sin_s = _rope_tables(jnp.tile(past + jnp.arange(tdec), nb))
    q_fs, ckv_s, kr_s, k_fs = _mla_qkv_call(xs, g1, msm[0], msm[1], cos_s, sin_s, w1, gq, gkv, w_lat, w_qr, w_qrr, 256)
    q_rows = q_fs.reshape(MLA_HEADS, nb, tdec, 2 * LANES).transpose(1, 2, 0, 3).reshape(nb, tdec * MLA_HEADS, 2 * LANES)
    trow_col = jnp.repeat(jnp.arange(tdec, dtype=F32), MLA_HEADS).reshape(tdec * MLA_HEADS, 1)
    k_new = jnp.pad(k_fs.reshape(nb, tdec, 2 * LANES), ((0, 0), (0, LANES - tdec), (0, 0)))
    o_lat_s = _mla_sample_call(page_table, q_rows, trow_col, k_new, cache_mla_ckv, cache_mla_krope.transpose(0, 2, 1))
    xs = _mm_res(o_lat_s.reshape(ms, MLA_HEADS * LANES), w_vo, xs, msm[2], 256)
    y_sample = moe(xs, 1, msm, 512, True).reshape(nb, tdec, d)

    return (y_prompt, y_sample, moba_k_prompt, moba_v_prompt,
            ckv_p.reshape(1, seq, MLA_KV_LORA), kr_p.reshape(1, seq, MLA_ROPE),
            moba_k_sample, moba_v_sample,
            ckv_s.reshape(nb, tdec, MLA_KV_LORA), kr_s.reshape(nb, tdec, MLA_ROPE))
```

```python
import functools

import jax
import jax.numpy as jnp
from jax import lax
from jax.experimental import pallas as pl
from jax.experimental.pallas import tpu as pltpu

F32 = jnp.float32
BF16 = jnp.bfloat16
HIGHEST = lax.Precision.HIGHEST

D_MODEL = 1024
N_ADA = 6
NORM_EPS = 1e-6
MOBA_HEADS = 16
MOBA_KV_HEADS = 4
MOBA_GROUP = MOBA_HEADS // MOBA_KV_HEADS
MOBA_HEAD_DIM = 64
MOBA_BLOCK = 256
MOBA_TOPK = 3
MLA_HEADS = 16
MLA_Q_LORA = 256
MLA_KV_LORA = 128
MLA_NOPE = 64
MLA_ROPE = 32
MLA_V = 64
ROPE_THETA = 10000.0
N_GROUPS = 4
EXPERTS_PER_GROUP = 8
N_EXPERTS = N_GROUPS * EXPERTS_PER_GROUP
D_EXPERT = 256

LANES = 128
NEG = -1e30
LOG2E = 1.4426950408889634
VMEM_LIMIT = 56 * 1024 * 1024
MOBA_PAGES_PER_STEP = 32
PAGES_PER_STEP = 16

_NT = (((1,), (1,)), ((), ()))


def _nt_dot(a, b, **kw):
    return lax.dot_general(a, b, _NT, preferred_element_type=F32, **kw)


def _cparams(*sem):
    return pltpu.CompilerParams(dimension_semantics=sem, vmem_limit_bytes=VMEM_LIMIT)


def _rms_mod(x, g, shift, scale):
    y = x * lax.rsqrt(jnp.mean(x * x, axis=-1, keepdims=True) + NORM_EPS)
    return (y * g) * (1.0 + scale) + shift


def _top_mask_bias(vals, lanef, n_top, bias):
    for _ in range(n_top):
        m = jnp.max(vals, axis=-1, keepdims=True)
        idx = jnp.min(jnp.where(vals == m, lanef, 1e9), axis=-1, keepdims=True)
        idx = jnp.where(m > -jnp.inf, idx, -1.0)
        pick = lanef == idx
        bias = jnp.where(pick, 0.0, bias)
        vals = jnp.where(pick, -jnp.inf, vals)
    return bias


def _ada_kernel(c_ref, w_ref, b_ref, o_ref):
    c = c_ref[...]
    a = c * jax.nn.sigmoid(c)
    o_ref[0] = jnp.dot(a, w_ref[0], preferred_element_type=F32, precision=HIGHEST) + b_ref[0]


def _ada_call(c_all, w_ada, b_ada):
    depth, d, n = w_ada.shape
    rows = c_all.shape[0]
    tn = 512
    return pl.pallas_call(
        _ada_kernel,
        grid=(depth, n // tn),
        in_specs=[
            pl.BlockSpec((rows, d), lambda l, j: (0, 0)),
            pl.BlockSpec((1, d, tn), lambda l, j: (l, 0, j)),
            pl.BlockSpec((1, 1, tn), lambda l, j: (l, 0, j)),
        ],
        out_specs=pl.BlockSpec((1, rows, tn), lambda l, j: (l, 0, j)),
        out_shape=jax.ShapeDtypeStruct((depth, rows, n), F32),
        compiler_params=_cparams("arbitrary", "arbitrary"),
        name="ada_mod",
    )(c_all, w_ada, b_ada.reshape(depth, 1, n))


def _mod_spec(mod, tm):
    d = mod.shape[1]
    if mod.shape[0] == 1:
        return pl.BlockSpec((1, d), lambda i, *_: (0, 0))
    return pl.BlockSpec((tm, d), lambda i, *_: (i, 0))


def _norm_matmul_kernel(x_ref, g_ref, sh_ref, sc_ref, w_ref, o_ref):
    h = _rms_mod(x_ref[...], g_ref[...], sh_ref[...], sc_ref[...]).astype(BF16)
    o_ref[...] = jnp.dot(h, w_ref[...], preferred_element_type=F32)


def _norm_matmul(x, g, shift, scale, w, tm):
    m, d = x.shape
    tm = min(tm, m)
    n = w.shape[1]
    return pl.pallas_call(
        _norm_matmul_kernel,
        grid=(m // tm,),
        in_specs=[
            pl.BlockSpec((tm, d), lambda i: (i, 0)),
            pl.BlockSpec((1, d), lambda i: (0, 0)),
            _mod_spec(shift, tm),
            _mod_spec(scale, tm),
            pl.BlockSpec((d, n), lambda i: (0, 0)),
        ],
        out_specs=pl.BlockSpec((tm, n), lambda i: (i, 0)),
        out_shape=jax.ShapeDtypeStruct((m, n), F32),
        compiler_params=_cparams("arbitrary"),
        name="norm_matmul",
    )(x, g, shift, scale, w)


def _mm_res_kernel(a_ref, w_ref, x_ref, gate_ref, o_ref):
    y = jnp.dot(a_ref[...], w_ref[...], preferred_element_type=F32)
    o_ref[...] = x_ref[...] + gate_ref[...] * y


def _mm_res(a, w, x, gate, tm):
    m, k = a.shape
    tm = min(tm, m)
    n = w.shape[1]
    return pl.pallas_call(
        _mm_res_kernel,
        grid=(m // tm,),
        in_specs=[
            pl.BlockSpec((tm, k), lambda i: (i, 0)),
            pl.BlockSpec((k, n), lambda i: (0, 0)),
            pl.BlockSpec((tm, n), lambda i: (i, 0)),
            _mod_spec(gate, tm),
        ],
        out_specs=pl.BlockSpec((tm, n), lambda i: (i, 0)),
        out_shape=jax.ShapeDtypeStruct((m, n), F32),
        compiler_params=_cparams("arbitrary"),
        name="matmul_residual",
    )(a, w, x, gate)


def _bmm_kernel(a_ref, b_ref, o_ref):
    o_ref[0] = jnp.dot(a_ref[0], b_ref[0], preferred_element_type=F32, precision=HIGHEST)


def _bmm(a, b):
    n, m, k = a.shape
    p = b.shape[2]
    return pl.pallas_call(
        _bmm_kernel,
        grid=(n,),
        in_specs=[pl.BlockSpec((1, m, k), lambda i: (i, 0, 0)), pl.BlockSpec((1, k, p), lambda i: (i, 0, 0))],
        out_specs=pl.BlockSpec((1, m, p), lambda i: (i, 0, 0)),
        out_shape=jax.ShapeDtypeStruct((n, m, p), F32),
        compiler_params=_cparams("arbitrary"),
        name="fold_weights",
    )(a, b)


_QW = MOBA_HEADS * LANES
_KW = MOBA_KV_HEADS * LANES


def _moba_qkv_kernel(x_ref, g_ref, sh_ref, sc_ref, w_ref, q_ref, ka_ref, vp_ref, ko_ref, vo_ref, km_ref):
    i = pl.program_id(0)
    tm = x_ref.shape[0]
    half = LANES // 2

    @pl.when(i == 0)
    def _():
        km_ref[...] = jnp.zeros_like(km_ref)

    h = _rms_mod(x_ref[...], g_ref[...], sh_ref[...], sc_ref[...]).astype(BF16)
    y = jnp.dot(h, w_ref[...], preferred_element_type=F32)
    lane = lax.broadcasted_iota(jnp.int32, (tm, LANES), 1)
    lanef = lane.astype(F32)
    kvw = MOBA_KV_HEADS * MOBA_HEAD_DIM
    ko_ref[...] = y[:, _QW + 2 * _KW:_QW + 2 * _KW + kvw]
    vo_ref[...] = y[:, _QW + 2 * _KW + kvw:_QW + 2 * _KW + 2 * kvw]
    for k in range(MOBA_KV_HEADS):
        kp = y[:, _QW + k * LANES:_QW + (k + 1) * LANES]
        km_ref[k, pl.ds(half + i, 1), :] = jnp.mean(kp, axis=0, keepdims=True)
        ka_ref[:, k * LANES:(k + 1) * LANES] = jnp.where(lane == half + i, 1.0, kp).astype(BF16)
        vp = y[:, _QW + _KW + k * LANES:_QW + _KW + (k + 1) * LANES]
        vp_ref[:, k * LANES:(k + 1) * LANES] = jnp.where(lane == LANES - 1, 1.0, vp).astype(BF16)
    past = (lane >= half) & (lane < half + i)
    own_bias = jnp.where(lane == half + i, 0.0, NEG)
    for hd in range(MOBA_HEADS):
        qp = y[:, hd * LANES:(hd + 1) * LANES]
        gs = _nt_dot(qp, km_ref[hd // MOBA_GROUP], precision=HIGHEST)
        bias = _top_mask_bias(jnp.where(past, gs, -jnp.inf), lanef, MOBA_TOPK, own_bias)
        q_ref[hd] = jnp.where(lane < half, qp * (MOBA_HEAD_DIM ** -0.5 * LOG2E), bias).astype(BF16)


def _moba_qkv_call(x, g, shift, scale, w):
    t, d = x.shape
    tm = MOBA_BLOCK
    assert t // tm <= LANES // 2
    n = w.shape[1]
    kvw = MOBA_KV_HEADS * MOBA_HEAD_DIM
    return pl.pallas_call(
        _moba_qkv_kernel,
        grid=(t // tm,),
        in_specs=[
            pl.BlockSpec((tm, d), lambda i: (i, 0)),
            pl.BlockSpec((1, d), lambda i: (0, 0)),
            pl.BlockSpec((1, d), lambda i: (0, 0)),
            pl.BlockSpec((1, d), lambda i: (0, 0)),
            pl.BlockSpec((d, n), lambda i: (0, 0)),
        ],
        out_specs=[
            pl.BlockSpec((MOBA_HEADS, tm, LANES), lambda i: (0, i, 0)),
            pl.BlockSpec((tm, _KW), lambda i: (i, 0)),
            pl.BlockSpec((tm, _KW), lambda i: (i, 0)),
            pl.BlockSpec((tm, kvw), lambda i: (i, 0)),
            pl.BlockSpec((tm, kvw), lambda i: (i, 0)),
        ],
        out_shape=[
            jax.ShapeDtypeStruct((MOBA_HEADS, t, LANES), BF16),
            jax.ShapeDtypeStruct((t, _KW), BF16),
            jax.ShapeDtypeStruct((t, _KW), BF16),
            jax.ShapeDtypeStruct((t, kvw), F32),
            jax.ShapeDtypeStruct((t, kvw), F32),
        ],
        scratch_shapes=[pltpu.VMEM((MOBA_KV_HEADS, LANES, LANES), F32)],
        compiler_params=_cparams("arbitrary"),
        name="moba_qkv_gate",
    )(x, g, shift, scale, w)


def _moba_attn_kernel(q_ref, k_ref, v_ref, al_ref, sl_ref, o_ref, m_sc, acc_sc):
    i = pl.program_id(1)
    tq = q_ref.shape[1]
    blk = MOBA_BLOCK
    n_chains, rows, _ = m_sc.shape
    hpc = rows // tq
    m_sc[...] = jnp.full_like(m_sc, NEG)
    acc_sc[...] = jnp.zeros_like(acc_sc)

    def step(j, nblk, masked):
        tk = nblk * blk
        start = pl.multiple_of(j * blk, blk)
        dj = ((j - i) * blk).astype(F32)
        for u in range(n_chains):
            kv = u * hpc // MOBA_GROUP
            kb = k_ref[pl.ds(start, tk), kv * LANES:(kv + 1) * LANES]
            vb = v_ref[pl.ds(start, tk), kv * LANES:(kv + 1) * LANES]
            q = q_ref[u * hpc:(u + 1) * hpc].reshape(rows, LANES)
            s = _nt_dot(q, kb) + al_ref[u, :, :tk]
            if masked:
                row_pos = (nblk - 1) * blk + (lax.broadcasted_iota(jnp.int32, (rows, tk), 0) & (tq - 1))
                col = lax.broadcasted_iota(jnp.int32, (rows, tk), 1)
                s = jnp.where(col <= row_pos, s, NEG)
            c = sl_ref[u] * dj
            m_prev = m_sc[u]
            m_new = jnp.maximum(m_prev, jnp.max(s, axis=-1, keepdims=True) + c)
            shift = m_new - c
            p = jnp.exp2(s - jnp.concatenate([shift] * (tk // LANES), axis=1))
            acc_sc[u] = acc_sc[u] * jnp.exp2(m_prev - m_new) + jnp.dot(p.astype(BF16), vb, preferred_element_type=F32)
            m_sc[u] = m_new

    def past(jj, _):
        step(2 * jj, 2, False)
        return 0

    lax.fori_loop(0, i // 2, past, 0)

    @pl.when(i % 2 == 1)
    def _():
        step(i - 1, 2, True)

    @pl.when(i % 2 == 0)
    def _():
        step(i, 1, True)
    for u in range(n_chains):
        acc = acc_sc[u]
        o = acc / acc[:, LANES - 1:]
        for x in range(hpc):
            hd = u * hpc + x
            o_ref[:, hd * LANES:(hd + 1) * LANES] = o[x * tq:(x + 1) * tq].astype(BF16)


def _moba_attn_call(q_aug, k_aug, v_pad, al_tab, sl_tab):
    nh, t, _ = q_aug.shape
    tq = MOBA_BLOCK
    assert tq & (tq - 1) == 0
    kvps = 2
    n_steps = MOBA_KV_HEADS // kvps
    n_chains, rows = al_tab.shape[0] // n_steps, al_tab.shape[1]
    gw = kvps * MOBA_GROUP * LANES
    once = pl.Buffered(1)
    return pl.pallas_call(
        _moba_attn_kernel,
        grid=(n_steps, t // tq),
        in_specs=[
            pl.BlockSpec((kvps * MOBA_GROUP, tq, LANES), lambda k, i: (k, i, 0)),
            pl.BlockSpec((t, kvps * LANES), lambda k, i: (0, k), pipeline_mode=once),
            pl.BlockSpec((t, kvps * LANES), lambda k, i: (0, k), pipeline_mode=once),
            pl.BlockSpec((n_chains, rows, 2 * MOBA_BLOCK), lambda k, i: (k, 0, 0), pipeline_mode=once),
            pl.BlockSpec((n_chains, rows, LANES), lambda k, i: (k, 0, 0), pipeline_mode=once),
        ],
        out_specs=pl.BlockSpec((tq, gw), lambda k, i: (i, k)),
        out_shape=jax.ShapeDtypeStruct((t, nh * LANES), BF16),
        scratch_shapes=[pltpu.VMEM((n_chains, rows, LANES), F32), pltpu.VMEM((n_chains, rows, LANES), F32)],
        compiler_params=_cparams("arbitrary", "arbitrary"),
        name="moba_attn",
    )(q_aug, k_aug, v_pad, al_tab, sl_tab)


def _moe_kernel(x_ref, g_ref, sh_ref, sc_ref, gt_ref, gf_ref, wg_ref, we_ref, wgu_ref, wdn_ref, o_ref,
                h_sc, gate_sc, acc_sc, *, final):
    e = pl.program_id(1)
    tm = x_ref.shape[0]
    lane = lax.broadcasted_iota(jnp.int32, (tm, LANES), 1)

    @pl.when(e == 0)
    def _():
        h = _rms_mod(x_ref[...], g_ref[...], sh_ref[...], sc_ref[...])
        h_sc[...] = h.astype(BF16)
        lanef = lane.astype(F32)
        gl = jnp.dot(h, wg_ref[...], preferred_element_type=F32, precision=HIGHEST)
        gl = jnp.where(lane < N_GROUPS, gl, -jnp.inf)
        gmax = jnp.max(gl, axis=-1, keepdims=True)
        g_sel = jnp.min(jnp.where(gl == gmax, lanef, 1e9), axis=-1, keepdims=True)
        g_p = 1.0 / jnp.sum(jnp.exp(gl - gmax), axis=-1, keepdims=True)
        el = jnp.dot(h, we_ref[...], preferred_element_type=F32, precision=HIGHEST)
        lo = g_sel * EXPERTS_PER_GROUP
        el = jnp.where((lanef >= lo) & (lanef < lo + EXPERTS_PER_GROUP), el, -jnp.inf)
        emax = jnp.max(el, axis=-1, keepdims=True)
        ex = jnp.exp(el - emax)
        prob = ex / jnp.sum(ex, axis=-1, keepdims=True)
        prob = jnp.where(el > -jnp.inf, prob, -jnp.inf)
        p1 = jnp.max(prob, axis=-1, keepdims=True)
        i1 = jnp.min(jnp.where(prob == p1, lanef, 1e9), axis=-1, keepdims=True)
        prob2 = jnp.where(lanef == i1, -jnp.inf, prob)
        p2 = jnp.max(prob2, axis=-1, keepdims=True)
        i2 = jnp.min(jnp.where(prob2 == p2, lanef, 1e9), axis=-1, keepdims=True)
        den = p1 + p2
        gate_sc[...] = jnp.where(lanef == i1, p1 / den * g_p, jnp.where(lanef == i2, p2 / den * g_p, 0.0))
        acc_sc[...] = jnp.zeros_like(acc_sc)

    gcol = jnp.sum(jnp.where(lane == e, gate_sc[...], 0.0), axis=-1, keepdims=True)
    gu = jnp.dot(h_sc[...], wgu_ref[0], preferred_element_type=F32)
    gg = gu[:, :D_EXPERT]
    a = (gg * jax.nn.sigmoid(gg)) * gu[:, D_EXPERT:] * gcol
    acc_sc[...] += jnp.dot(a.astype(BF16), wdn_ref[0], preferred_element_type=F32)

    @pl.when(e == pl.num_programs(1) - 1)
    def _():
        y = x_ref[...] + gt_ref[...] * acc_sc[...]
        if final:
            y = y * lax.rsqrt(jnp.mean(y * y, axis=-1, keepdims=True) + NORM_EPS) * gf_ref[...]
        o_ref[...] = y


def _moe_call(x, g, shift, scale, gate, g_final, w_group, w_expert, w_gu, w_down, tm, final):
    m, d = x.shape
    tm = min(tm, m)
    ne, _, f2 = w_gu.shape
    fd = w_down.shape[1]
    return pl.pallas_call(
        functools.partial(_moe_kernel, final=final),
        grid=(m // tm, ne),
        in_specs=[
            pl.BlockSpec((tm, d), lambda i, e: (i, 0)),
            pl.BlockSpec((1, d), lambda i, e: (0, 0)),
            _mod_spec(shift, tm),
            _mod_spec(scale, tm),
            _mod_spec(gate, tm),
            pl.BlockSpec((1, d), lambda i, e: (0, 0)),
            pl.BlockSpec((d, LANES), lambda i, e: (0, 0)),
            pl.BlockSpec((d, LANES), lambda i, e: (0, 0)),
            pl.BlockSpec((1, d, f2), lambda i, e: (e, 0, 0)),
            pl.BlockSpec((1, fd, d), lambda i, e: (e, 0, 0)),
        ],
        out_specs=pl.BlockSpec((tm, d), lambda i, e: (i, 0)),
        out_shape=jax.ShapeDtypeStruct((m, d), F32),
        scratch_shapes=[pltpu.VMEM((tm, d), BF16), pltpu.VMEM((tm, LANES), F32), pltpu.VMEM((tm, d), F32)],
        compiler_params=_cparams("arbitrary", "arbitrary"),
        name="hier_moe",
    )(x, g, shift, scale, gate, g_final, w_group, w_expert, w_gu, w_down)


_MLA_W1 = MLA_Q_LORA + MLA_KV_LORA + 2 * LANES


def _mla_qkv_kernel(x_ref, g_ref, sh_ref, sc_ref, cos_ref, sin_ref, w1_ref, gq_ref, gkv_ref, wl_ref, wr_ref, wrr_ref,
                    q_ref, ckv_ref, kr_ref, kf_ref):
    h = _rms_mod(x_ref[...], g_ref[...], sh_ref[...], sc_ref[...]).astype(BF16)
    a = jnp.dot(h, w1_ref[...], preferred_element_type=F32)
    cos = cos_ref[...]
    sin = sin_ref[...]
    cq = a[:, :MLA_Q_LORA]
    cqn = (cq * lax.rsqrt(jnp.mean(cq * cq, axis=-1, keepdims=True) + NORM_EPS) * gq_ref[...]).astype(BF16)
    kvc = a[:, MLA_Q_LORA:MLA_Q_LORA + MLA_KV_LORA]
    ckv = kvc * lax.rsqrt(jnp.mean(kvc * kvc, axis=-1, keepdims=True) + NORM_EPS) * gkv_ref[...]
    o = MLA_Q_LORA + MLA_KV_LORA
    kr = a[:, o:o + LANES] * cos + a[:, o + LANES:o + 2 * LANES] * sin
    ckv_ref[...] = ckv
    kr_ref[...] = kr[:, :MLA_ROPE]
    kf_ref[:, :LANES] = ckv.astype(BF16)
    lane = lax.broadcasted_iota(jnp.int32, kr.shape, 1)
    kf_ref[:, LANES:] = jnp.where(lane == LANES - 1, 1.0, kr).astype(BF16)
    scale = (MLA_NOPE + MLA_ROPE) ** -0.5 * LOG2E
    ql = jnp.dot(cqn, wl_ref[...], preferred_element_type=F32)
    qr = jnp.dot(cqn, wr_ref[...], preferred_element_type=F32)
    qrr = jnp.dot(cqn, wrr_ref[...], preferred_element_type=F32)
    for hd in range(MLA_HEADS):
        sl = slice(hd * LANES, (hd + 1) * LANES)
        q_ref[hd, :, :LANES] = (ql[:, sl] * scale).astype(BF16)
        q_ref[hd, :, LANES:] = ((qr[:, sl] * cos + qrr[:, sl] * sin) * scale).astype(BF16)


def _mla_qkv_call(x, g, shift, scale, cos, sin, w1, gq, gkv, wl, wr, wrr, tm):
    m, d = x.shape
    tm = min(tm, m)
    hw = MLA_HEADS * LANES
    full = lambda shape: pl.BlockSpec(shape, lambda i: (0,) * len(shape))
    return pl.pallas_call(
        _mla_qkv_kernel,
        grid=(m // tm,),
        in_specs=[
            pl.BlockSpec((tm, d), lambda i: (i, 0)),
            full((1, d)),
            _mod_spec(shift, tm),
            _mod_spec(scale, tm),
            pl.BlockSpec((tm, LANES), lambda i: (i, 0)),
            pl.BlockSpec((tm, LANES), lambda i: (i, 0)),
            full((d, _MLA_W1)),
            full((1, MLA_Q_LORA)),
            full((1, MLA_KV_LORA)),
            full((MLA_Q_LORA, hw)),
            full((MLA_Q_LORA, hw)),
            full((MLA_Q_LORA, hw)),
        ],
        out_specs=[
            pl.BlockSpec((MLA_HEADS, tm, 2 * LANES), lambda i: (0, i, 0)),
            pl.BlockSpec((tm, MLA_KV_LORA), lambda i: (i, 0)),
            pl.BlockSpec((tm, MLA_ROPE), lambda i: (i, 0)),
            pl.BlockSpec((tm, 2 * LANES), lambda i: (i, 0)),
        ],
        out_shape=[
            jax.ShapeDtypeStruct((MLA_HEADS, m, 2 * LANES), BF16),
            jax.ShapeDtypeStruct((m, MLA_KV_LORA), F32),
            jax.ShapeDtypeStruct((m, MLA_ROPE), F32),
            jax.ShapeDtypeStruct((m, 2 * LANES), BF16),
        ],
        compiler_params=_cparams("arbitrary"),
        name="mla_qkv",
    )(x, g, shift, scale, cos, sin, w1, gq, gkv, wl, wr, wrr)


def _mla_attn_kernel(q_ref, k_ref, o_ref, m_sc, acc_sc, *, tk, hpc, cpb):
    i = pl.program_id(0)
    nh, tq, w = q_ref.shape
    rows = hpc * tq
    n_chains = nh // hpc
    n_past = (i * tq) // tk
    m_sc[...] = jnp.full_like(m_sc, NEG)
    acc_sc[...] = jnp.zeros_like(acc_sc)

    def tile(j, masked):
        start = pl.multiple_of(j * tk, tk)
        kb = k_ref[pl.ds(start, tk), :]

        def chain_pair(cp, _):
            for u in range(cpb):
                c = cp * cpb + u
                q = q_ref[pl.ds(c * hpc, hpc)].reshape(rows, w)
                s = _nt_dot(q, kb)
                if masked:
                    row_pos = i * tq + (lax.broadcasted_iota(jnp.int32, (rows, tk), 0) & (tq - 1))
                    col_pos = start + lax.broadcasted_iota(jnp.int32, (rows, tk), 1)
                    s = jnp.where(col_pos <= row_pos, s, NEG)
                m_prev = m_sc[c]
                m_new = jnp.maximum(m_prev, jnp.max(s, axis=-1, keepdims=True))
                p = jnp.exp2(s - jnp.concatenate([m_new] * (tk // LANES), axis=1))
                alpha = jnp.exp2(m_prev - m_new)
                acc_sc[c] = (acc_sc[c] * jnp.concatenate([alpha] * (w // LANES), axis=1)
                             + jnp.dot(p.astype(BF16), kb, preferred_element_type=F32))
                m_sc[c] = m_new
            return 0

        lax.fori_loop(0, n_chains // cpb, chain_pair, 0)

    def past(j, _):
        tile(j, False)
        return 0

    lax.fori_loop(0, n_past, past, 0)
    tile(n_past, True)
    for c in range(n_chains):
        acc = acc_sc[c]
        o = acc[:, :LANES] / acc[:, w - 1:]
        for x in range(hpc):
            hd = c * hpc + x
            o_ref[:, hd * LANES:(hd + 1) * LANES] = o[x * tq:(x + 1) * tq].astype(BF16)


def _mla_attn_call(q_full, k_full, tq, tk, hpc, cpb):
    nh, t, w = q_full.shape
    assert tq & (tq - 1) == 0 and tk % tq == 0 and t % tk == 0 and nh % (cpb * hpc) == 0
    n_chains, rows = nh // hpc, hpc * tq
    return pl.pallas_call(
        functools.partial(_mla_attn_kernel, tk=tk, hpc=hpc, cpb=cpb),
        grid=(t // tq,),
        in_specs=[
            pl.BlockSpec((nh, tq, w), lambda i: (0, i, 0)),
            pl.BlockSpec((t, w), lambda i: (0, 0)),
        ],
        out_specs=pl.BlockSpec((tq, nh * LANES), lambda i: (i, 0)),
        out_shape=jax.ShapeDtypeStruct((t, nh * LANES), BF16),
        scratch_shapes=[pltpu.VMEM((n_chains, rows, LANES), F32), pltpu.VMEM((n_chains, rows, w), F32)],
        compiler_params=_cparams("arbitrary"),
        name="mla_attn",
    )(q_full, k_full)


def _moba_sample_kernel(pt_ref, q_ref, slope_ref, tpos_ref, hk_ref, kn_ref, vn_ref, k_hbm, v_hbm, o_ref,
                        kbuf, vbuf, sem, s_sc, p_sc, g_sc, b_sc, pn_sc, l_sc, *, n_pages, past_len):
    pps = MOBA_PAGES_PER_STEP
    seq = pl.program_id(0)
    n_seq = pl.num_programs(0)
    n_chunks = n_pages // pps
    n_blocks = n_pages // 2
    q = q_ref[0]
    rows = q.shape[0]
    lane = lax.broadcasted_iota(jnp.int32, (rows, LANES), 1)
    lanef = lane.astype(F32)

    def chunk_copies(hbm, buf, which, sq, c, slot):
        return [pltpu.make_async_copy(hbm.at[pt_ref[sq, c * pps + p]], buf.at[slot, p], sem.at[which, slot])
                for p in range(pps)]

    def start_k(sq, c, slot):
        for cp in chunk_copies(k_hbm, kbuf, 0, sq, c, slot):
            cp.start()

    def start_v(sq, c, slot):
        for cp in chunk_copies(v_hbm, vbuf, 1, sq, c, slot):
            cp.start()

    @pl.when(seq == 0)
    def _():
        start_k(0, 0, 0)

    for c in range(n_chunks):
        slot = c % 2
        for cp in chunk_copies(k_hbm, kbuf, 0, seq, c, slot):
            cp.wait()
        if c + 1 < n_chunks:
            start_k(seq, c + 1, 1 - slot)
        else:
            start_v(seq, 0, 0)
        for a in range(pps // 2):
            kb = jnp.concatenate([kbuf[slot, 2 * a].astype(BF16), kbuf[slot, 2 * a + 1].astype(BF16)], axis=1)
            s2 = jnp.dot(q, kb, preferred_element_type=F32)
            s_sc[c * pps + 2 * a] = s2[:, :LANES]
            s_sc[c * pps + 2 * a + 1] = s2[:, LANES:]

    def softmax():
        for b in range(n_blocks):
            r = jnp.sum(s_sc[2 * b] + s_sc[2 * b + 1], axis=-1, keepdims=True)
            g_sc[b] = jnp.broadcast_to(r, (rows, LANES))
            b_sc[b] = jnp.full((rows, LANES), NEG, F32)
        for _ in range(min(MOBA_TOPK, n_blocks)):
            best = g_sc[0]
            for b in range(1, n_blocks):
                best = jnp.maximum(best, g_sc[b])
            idx = jnp.full((rows, LANES), float(n_blocks), F32)
            for b in reversed(range(n_blocks)):
                idx = jnp.where(g_sc[b] == best, float(b), idx)
            for b in range(n_blocks):
                hit = idx == float(b)
                b_sc[b] = jnp.where(hit, 0.0, b_sc[b])
                g_sc[b] = jnp.where(hit, -jnp.inf, g_sc[b])
        slope = jnp.broadcast_to(slope_ref[...], (rows, LANES))
        tpos = jnp.broadcast_to(tpos_ref[...], (rows, LANES))
        al_lane = slope * lanef
        sn = _nt_dot(q, kn_ref[0]) - slope * (tpos - (past_len + lanef))
        sn = jnp.where(past_len + lanef <= tpos, sn, NEG)
        mv = sn
        for b in range(n_blocks):
            for u in range(2):
                pg = 2 * b + u
                sc = s_sc[pg] + al_lane + (b_sc[b] + slope * (float(pg * LANES) - tpos))
                s_sc[pg] = sc
                mv = jnp.maximum(mv, sc)
        m = jnp.broadcast_to(jnp.max(mv, axis=-1, keepdims=True), (rows, LANES))
        pn = jnp.exp(sn - m)
        lv = pn
        for pg in range(n_pages):
            p = jnp.exp(s_sc[pg] - m)
            p_sc[pg] = p.astype(BF16)
            lv = lv + p
        l_sc[...] = jnp.sum(lv, axis=-1, keepdims=True)
        pn_sc[...] = pn.astype(BF16)

    softmax()
    acc = jnp.dot(pn_sc[...], vn_ref[0], preferred_element_type=F32)
    for c in range(n_chunks):
        slot = c % 2
        for cp in chunk_copies(v_hbm, vbuf, 1, seq, c, slot):
            cp.wait()
        if c + 1 < n_chunks:
            start_v(seq, c + 1, 1 - slot)
        else:
            @pl.when(seq + 1 < n_seq)
            def _():
                start_k(seq + 1, 0, 0)
        for a in range(pps // 2):
            pg = c * pps + 2 * a
            p2 = jnp.concatenate([p_sc[pg], p_sc[pg + 1]], axis=1)
            vt = jnp.concatenate([vbuf[slot, 2 * a].astype(BF16), vbuf[slot, 2 * a + 1].astype(BF16)], axis=1)
            acc += _nt_dot(p2, vt)
    o_ref[0] = (acc / l_sc[...] * hk_ref[...]).astype(BF16)


def _moba_sample_call(page_table, qbd, slope_col, tpos_col, hk_mask, k_new, v_new, cache_k, cache_v):
    nb, n_pages = page_table.shape
    rows, w = qbd.shape[1:]
    page = cache_k.shape[2]
    pps = MOBA_PAGES_PER_STEP
    assert page == LANES and n_pages % pps == 0 and pps % 2 == 0

    per_seq = lambda shape: pl.BlockSpec((1,) + shape, lambda b, pt: (b, 0, 0))
    const = lambda shape: pl.BlockSpec(shape, lambda b, pt: (0, 0))
    hbm = pl.BlockSpec(memory_space=pl.ANY)
    grid_spec = pltpu.PrefetchScalarGridSpec(
        num_scalar_prefetch=1,
        grid=(nb,),
        in_specs=[per_seq((rows, w)), const((rows, 1)), const((rows, 1)), const((rows, w)),
                  per_seq((LANES, w)), per_seq((LANES, w)), hbm, hbm],
        out_specs=per_seq((rows, w)),
        scratch_shapes=[
            pltpu.VMEM((2, pps, w, page), F32),
            pltpu.VMEM((2, pps, w, page), F32),
            pltpu.SemaphoreType.DMA((2, 2)),
            pltpu.VMEM((n_pages, rows, LANES), F32),
            pltpu.VMEM((n_pages, rows, LANES), BF16),
            pltpu.VMEM((n_pages // 2, rows, LANES), F32),
            pltpu.VMEM((n_pages // 2, rows, LANES), F32),
            pltpu.VMEM((rows, LANES), BF16),
            pltpu.VMEM((rows, 1), F32),
        ],
    )
    return pl.pallas_call(
        functools.partial(_moba_sample_kernel, n_pages=n_pages, past_len=n_pages * page),
        grid_spec=grid_spec,
        out_shape=jax.ShapeDtypeStruct((nb, rows, w), BF16),
        compiler_params=_cparams("arbitrary"),
        name="moba_sample_attn",
    )(page_table, qbd, slope_col, tpos_col, hk_mask, k_new, v_new, cache_k, cache_v)


def _mla_sample_kernel(pt_ref, q_ref, trow_ref, kn_ref, c_hbm, r_hbm, o_ref, cbuf, rbuf, sem, cb_sc, s_sc, *, n_pages):
    cpp = PAGES_PER_STEP
    n_chunks = n_pages // cpp
    kw = cpp * LANES
    seq = pl.program_id(0)
    n_seq = pl.num_programs(0)
    slot = seq % 2
    q = q_ref[0]
    ql = q[:, :LANES]
    qr = q[:, LANES:LANES + MLA_ROPE]
    rows = q.shape[0]

    def seq_copies(sq, sl):
        out = []
        for p in range(n_pages):
            pid = pt_ref[sq, p]
            out.append(pltpu.make_async_copy(c_hbm.at[pid], cbuf.at[sl, p], sem.at[0, sl]))
            out.append(pltpu.make_async_copy(r_hbm.at[pid], rbuf.at[sl, p], sem.at[1, sl]))
        return out

    @pl.when(seq == 0)
    def _():
        for cp in seq_copies(0, 0):
            cp.start()

    for cp in seq_copies(seq, slot):
        cp.wait()

    @pl.when(seq + 1 < n_seq)
    def _():
        for cp in seq_copies(seq + 1, 1 - slot):
            cp.start()

    mv = jnp.full((rows, kw), NEG, F32)
    for c in range(n_chunks):
        cb = cbuf[slot, c * cpp:(c + 1) * cpp].reshape(kw, MLA_KV_LORA).astype(BF16)
        cb_sc[c] = cb
        rb = jnp.concatenate([rbuf[slot, c * cpp + p].astype(BF16) for p in range(cpp)], axis=1)
        s = _nt_dot(ql, cb) + jnp.dot(qr, rb, preferred_element_type=F32)
        s_sc[c] = s
        mv = jnp.maximum(mv, s)
    kn = kn_ref[0]
    lanef = lax.broadcasted_iota(jnp.int32, (rows, LANES), 1).astype(F32)
    sn = jnp.where(lanef <= trow_ref[...], _nt_dot(q, kn), NEG)
    m = jnp.maximum(jnp.max(mv, axis=-1, keepdims=True), jnp.max(sn, axis=-1, keepdims=True))
    pn = jnp.exp2(sn - m)
    acc = jnp.dot(pn.astype(BF16), kn[:, :LANES], preferred_element_type=F32)
    lv = jnp.zeros((rows, kw), F32)
    for c in range(n_chunks):
        pr = jnp.exp2(s_sc[c] - m)
        lv = lv + pr
        acc = acc + jnp.dot(pr.astype(BF16), cb_sc[c], preferred_element_type=F32)
    l = jnp.sum(lv, axis=-1, keepdims=True) + jnp.sum(pn, axis=-1, keepdims=True)
    o_ref[0] = (acc / l).astype(BF16)


def _mla_sample_call(page_table, q_rows, trow_col, k_new, cache_ckv, cache_krope):
    nb, n_pages = page_table.shape
    rows, w = q_rows.shape[1:]
    page = cache_ckv.shape[1]
    cpp = PAGES_PER_STEP
    assert page == LANES and n_pages % cpp == 0
    n_chunks = n_pages // cpp

    per_seq = lambda shape: pl.BlockSpec((1,) + shape, lambda b, pt: (b, 0, 0))
    hbm = pl.BlockSpec(memory_space=pl.ANY)
    grid_spec = pltpu.PrefetchScalarGridSpec(
        num_scalar_prefetch=1,
        grid=(nb,),
        in_specs=[per_seq((rows, w)), pl.BlockSpec((rows, 1), lambda b, pt: (0, 0)), per_seq((LANES, w)), hbm, hbm],
        out_specs=per_seq((rows, LANES)),
        scratch_shapes=[
            pltpu.VMEM((2, n_pages, page, MLA_KV_LORA), F32),
            pltpu.VMEM((2, n_pages, MLA_ROPE, page), F32),
            pltpu.SemaphoreType.DMA((2, 2)),
            pltpu.VMEM((n_chunks, cpp * page, MLA_KV_LORA), BF16),
            pltpu.VMEM((n_chunks, rows, cpp * page), F32),
        ],
    )
    return pl.pallas_call(
        functools.partial(_mla_sample_kernel, n_pages=n_pages),
        grid_spec=grid_spec,
        out_shape=jax.ShapeDtypeStruct((nb, rows, LANES), BF16),
        compiler_params=_cparams("arbitrary"),
        name="mla_sample_attn",
    )(page_table, q_rows, trow_col, k_new, cache_ckv, cache_krope)


def _pad_last(w, width):
    return jnp.pad(w, [(0, 0)] * (w.ndim - 1) + [(0, width - w.shape[-1])])


def _moba_weights(w_qkv, w_o):
    d = w_qkv.shape[0]
    nq = MOBA_HEADS * MOBA_HEAD_DIM
    nk = MOBA_KV_HEADS * MOBA_HEAD_DIM
    wq = _pad_last(w_qkv[:, :nq].reshape(d, MOBA_HEADS, MOBA_HEAD_DIM), LANES).reshape(d, _QW)
    wk = w_qkv[:, nq:nq + nk]
    wv = w_qkv[:, nq + nk:]
    wkp = _pad_last(wk.reshape(d, MOBA_KV_HEADS, MOBA_HEAD_DIM), LANES).reshape(d, _KW)
    wvp = _pad_last(wv.reshape(d, MOBA_KV_HEADS, MOBA_HEAD_DIM), LANES).reshape(d, _KW)
    w_fused = jnp.concatenate([wq, wkp, wvp, wk, wv], axis=1).astype(BF16)
    wo = w_o.reshape(MOBA_HEADS, MOBA_HEAD_DIM, d)
    wo_prompt = jnp.pad(wo, ((0, 0), (0, LANES - MOBA_HEAD_DIM), (0, 0))).reshape(_QW, d).astype(BF16)
    kv_of = jnp.arange(MOBA_HEADS) // MOBA_GROUP
    onehot = (kv_of[:, None] == jnp.arange(MOBA_KV_HEADS)[None, :]).astype(F32)
    wo_sample = (onehot[:, :, None, None] * wo[:, None, :, :]).reshape(MOBA_HEADS * nk, d).astype(BF16)
    return w_fused, wo_prompt, wo_sample


def _rotate_half_cols(w):
    half = w.shape[-1] // 2
    return jnp.concatenate([-w[..., half:], w[..., :half]], axis=-1)


def _mla_weights(w_dq, w_uq, w_dkv, w_uk, w_uv, w_o):
    d = w_dq.shape[0]
    w_r = w_dkv[:, MLA_KV_LORA:]
    w1 = jnp.concatenate([w_dq, w_dkv[:, :MLA_KV_LORA], _pad_last(w_r, LANES), _pad_last(_rotate_half_cols(w_r), LANES)],
                         axis=1).astype(BF16)
    wq = w_uq.reshape(MLA_Q_LORA, MLA_HEADS, MLA_NOPE + MLA_ROPE)
    wq_nope = wq[:, :, :MLA_NOPE].transpose(1, 0, 2)
    wuk_t = w_uk.transpose(1, 2, 0)
    w_lat = _bmm(wq_nope, wuk_t).transpose(1, 0, 2).reshape(MLA_Q_LORA, MLA_HEADS * LANES).astype(BF16)
    wq_rope = wq[:, :, MLA_NOPE:]
    w_qr = _pad_last(wq_rope, LANES).reshape(MLA_Q_LORA, MLA_HEADS * LANES).astype(BF16)
    w_qrr = _pad_last(_rotate_half_cols(wq_rope), LANES).reshape(MLA_Q_LORA, MLA_HEADS * LANES).astype(BF16)
    wuv = w_uv.transpose(1, 0, 2)
    wo = w_o.reshape(MLA_HEADS, MLA_V, d)
    w_vo = _bmm(wuv, wo).reshape(MLA_HEADS * MLA_KV_LORA, d).astype(BF16)
    return w1, w_lat, w_qr, w_qrr, w_vo


def _rope_tables(pos):
    inv = ROPE_THETA ** (-jnp.arange(0, MLA_ROPE, 2, dtype=F32) / MLA_ROPE)
    ang = pos.astype(F32)[:, None] * inv[None, :]
    cos = jnp.cos(ang)
    sin = jnp.sin(ang)
    return _pad_last(jnp.concatenate([cos, cos], axis=-1), LANES), _pad_last(jnp.concatenate([sin, sin], axis=-1), LANES)


def kernel(x_prompt, x_sample, cache_moba_k, cache_moba_v, cache_mla_ckv, cache_mla_krope, page_table, c_prompt, c_sample, w_ada, b_ada, g_norm1, g_norm2, moba_w_qkv, moba_w_o, mla_w_dq, mla_g_q, mla_w_uq, mla_w_dkv, mla_g_kv, mla_w_uk, mla_w_uv, mla_w_o, moe_w_group, moe_w_expert, moe_w_gu, moe_w_down, g_final):
    bp, seq, d = x_prompt.shape
    assert bp == 1
    nb, tdec, _ = x_sample.shape
    n_pool, page = cache_moba_k.shape[:2]
    n_pages = page_table.shape[1]
    past = n_pages * page
    ms = nb * tdec
    xp = x_prompt.reshape(seq, d)
    xs = x_sample.reshape(ms, d)

    c_all = jnp.concatenate([c_sample, c_prompt, jnp.zeros((7, d), F32)], axis=0)
    ada = _ada_call(c_all, w_ada, b_ada).reshape(w_ada.shape[0], c_all.shape[0], N_ADA, d)

    def mods(layer):
        mp = [ada[layer, nb:nb + 1, i] for i in range(N_ADA)]
        msm = [jnp.repeat(ada[layer, :nb, i], tdec, axis=0) for i in range(N_ADA)]
        return mp, msm

    row = lambda v: v.reshape(1, -1)
    wg = [_pad_last(moe_w_group[l], LANES) for l in range(2)]
    we = [_pad_last(moe_w_expert[l], LANES) for l in range(2)]
    wgu = moe_w_gu.astype(BF16)
    wdn = moe_w_down.astype(BF16)
    gfin = row(g_final)

    def moe(x, layer, m, tm, final):
        return _moe_call(x, row(g_norm2[layer]), m[3], m[4], m[5], gfin, wg[layer], we[layer], wgu[layer], wdn[layer],
                         tm, final)

    mp, msm = mods(0)
    w_fused, wo_prompt, wo_sample = _moba_weights(moba_w_qkv, moba_w_o)
    g1 = row(g_norm1[0])
    slopes = jnp.exp2(-8.0 * jnp.arange(1, MOBA_HEADS + 1, dtype=F32) / MOBA_HEADS)

    q_aug, k_aug, v_pad, k_out, v_out = _moba_qkv_call(xp, g1, mp[0], mp[1], w_fused)
    hpc = 2
    slope_rows =jnp.repeat((slopes * LOG2E).reshape(MOBA_HEADS // hpc, hpc), MOBA_BLOCK, axis=1)
    al_tab = slope_rows[:, :, None] * jnp.arange(2 * MOBA_BLOCK, dtype=F32)
    sl_tab = jnp.broadcast_to(slope_rows[:, :, None], slope_rows.shape + (LANES,))
    o_pad = _moba_attn_call(q_aug, k_aug, v_pad, al_tab, sl_tab)
    xp = _mm_res(o_pad, wo_prompt, xp, mp[2], 512)
    xp = moe(xp, 0, mp, 1024, False)
    moba_k_prompt = k_out.reshape(1, seq, MOBA_KV_HEADS, MOBA_HEAD_DIM)
    moba_v_prompt = v_out.reshape(1, seq, MOBA_KV_HEADS, MOBA_HEAD_DIM)

    nq = MOBA_HEADS * MOBA_HEAD_DIM
    nk = MOBA_KV_HEADS * MOBA_HEAD_DIM
    qkv_s = _norm_matmul(xs, g1, msm[0], msm[1], moba_w_qkv.astype(BF16), 256)
    q_s = qkv_s[:, :nq].reshape(nb, tdec, MOBA_HEADS, 1, MOBA_HEAD_DIM) * MOBA_HEAD_DIM ** -0.5
    k_s = qkv_s[:, nq:nq + nk]
    v_s = qkv_s[:, nq + nk:]
    kv_of = jnp.arange(MOBA_HEADS) // MOBA_GROUP
    head_kv = (kv_of[:, None] == jnp.arange(MOBA_KV_HEADS)[None, :]).astype(F32)
    rows = tdec * MOBA_HEADS
    qbd = (q_s * head_kv[None, None, :, :, None]).reshape(nb, rows, nk).astype(BF16)
    slope_col = jnp.tile(slopes, tdec).reshape(rows, 1)
    tpos_col = jnp.repeat(past + jnp.arange(tdec, dtype=F32), MOBA_HEADS).reshape(rows, 1)
    hk_mask = jnp.tile(jnp.repeat(head_kv, MOBA_HEAD_DIM, axis=1), (tdec, 1))
    pad_new = lambda a: jnp.pad(a.reshape(nb, tdec, -1), ((0, 0), (0, LANES - tdec), (0, 0))).astype(BF16)
    om = _moba_sample_call(page_table, qbd, slope_col, tpos_col, hk_mask, pad_new(k_s), pad_new(v_s),
                           cache_moba_k.transpose(0, 2, 3, 1).reshape(n_pool, nk, page),
                           cache_moba_v.transpose(0, 2, 3, 1).reshape(n_pool, nk, page))
    xs = _mm_res(om.reshape(ms, MOBA_HEADS * nk), wo_sample, xs, msm[2], 256)
    xs = moe(xs, 0, msm, 512, False)
    moba_k_sample = k_s.reshape(nb, tdec, MOBA_KV_HEADS, MOBA_HEAD_DIM)
    moba_v_sample = v_s.reshape(nb, tdec, MOBA_KV_HEADS, MOBA_HEAD_DIM)

    mp, msm = mods(1)
    w1, w_lat, w_qr, w_qrr, w_vo = _mla_weights(mla_w_dq, mla_w_uq, mla_w_dkv, mla_w_uk, mla_w_uv, mla_w_o)
    g1 = row(g_norm1[1])
    gq = row(mla_g_q)
    gkv = row(mla_g_kv)

    cos_p, sin_p = _rope_tables(jnp.arange(seq))
    q_full, ckv_p, kr_p, k_full = _mla_qkv_call(xp, g1, mp[0], mp[1], cos_p, sin_p, w1, gq, gkv, w_lat, w_qr, w_qrr, 256)
    o_lat = _mla_attn_call(q_full, k_full, 256, 512, 2, 8)
    xp = _mm_res(o_lat, w_vo, xp, mp[2], 512)
    y_prompt = moe(xp, 1, mp, 1024, True).reshape(1, seq, d)

    cos_s, sin_s = _rope_tables(jnp.tile(past + jnp.arange(tdec), nb))
    q_fs, ckv_s, kr_s, k_fs = _mla_qkv_call(xs, g1, msm[0], msm[1], cos_s, sin_s, w1, gq, gkv, w_lat, w_qr, w_qrr, 256)
    q_rows = q_fs.reshape(MLA_HEADS, nb, tdec, 2 * LANES).transpose(1, 2, 0, 3).reshape(nb, tdec * MLA_HEADS, 2 * LANES)
    trow_col = jnp.repeat(jnp.arange(tdec, dtype=F32), MLA_HEADS).reshape(tdec * MLA_HEADS, 1)
    k_new = jnp.pad(k_fs.reshape(nb, tdec, 2 * LANES), ((0, 0), (0, LANES - tdec), (0, 0)))
    o_lat_s = _mla_sample_call(page_table, q_rows, trow_col, k_new, cache_mla_ckv, cache_mla_krope.transpose(0, 2, 1))
    xs = _mm_res(o_lat_s.reshape(ms, MLA_HEADS * LANES), w_vo, xs, msm[2], 256)
    y_sample = moe(xs, 1, msm, 512, True).reshape(nb, tdec, d)

    return (y_prompt, y_sample, moba_k_prompt, moba_v_prompt,
            ckv_p.reshape(1, seq, MLA_KV_LORA), kr_p.reshape(1, seq, MLA_ROPE),
            moba_k_sample, moba_v_sample,
            ckv_s.reshape(nb, tdec, MLA_KV_LORA), kr_s.reshape(nb, tdec, MLA_ROPE))
```

```python
import functools

import jax
import jax.numpy as jnp
from jax import lax
from jax.experimental import pallas as pl
from jax.experimental.pallas import tpu as pltpu

F32 = jnp.float32
BF16 = jnp.bfloat16
HIGHEST = lax.Precision.HIGHEST

D_MODEL = 1024
N_ADA = 6
NORM_EPS = 1e-6
MOBA_HEADS = 16
MOBA_KV_HEADS = 4
MOBA_GROUP = MOBA_HEADS // MOBA_KV_HEADS
MOBA_HEAD_DIM = 64
MOBA_BLOCK = 256
MOBA_TOPK = 3
MLA_HEADS = 16
MLA_Q_LORA = 256
MLA_KV_LORA = 128
MLA_NOPE = 64
MLA_ROPE = 32
MLA_V = 64
ROPE_THETA = 10000.0
N_GROUPS = 4
EXPERTS_PER_GROUP = 8
N_EXPERTS = N_GROUPS * EXPERTS_PER_GROUP
D_EXPERT = 256

LANES = 128
NEG = -1e30
LOG2E = 1.4426950408889634
VMEM_LIMIT = 56 * 1024 * 1024
MOBA_PAGES_PER_STEP = 32
PAGES_PER_STEP = 16

_NT = (((1,), (1,)), ((), ()))


def _nt_dot(a, b, **kw):
    return lax.dot_general(a, b, _NT, preferred_element_type=F32, **kw)


def _cparams(*sem):
    return pltpu.CompilerParams(dimension_semantics=sem, vmem_limit_bytes=VMEM_LIMIT)


def _rms_mod(x, g, shift, scale):
    y = x * lax.rsqrt(jnp.mean(x * x, axis=-1, keepdims=True) + NORM_EPS)
    return (y * g) * (1.0 + scale) + shift


def _top_mask_bias(vals, lanef, n_top, bias):
    for _ in range(n_top):
        m = jnp.max(vals, axis=-1, keepdims=True)
        idx = jnp.min(jnp.where(vals == m, lanef, 1e9), axis=-1, keepdims=True)
        idx = jnp.where(m > -jnp.inf, idx, -1.0)
        pick = lanef == idx
        bias = jnp.where(pick, 0.0, bias)
        vals = jnp.where(pick, -jnp.inf, vals)
    return bias


def _ada_kernel(c_ref, w_ref, b_ref, o_ref):
    c = c_ref[...]
    a = c * jax.nn.sigmoid(c)
    o_ref[0] = jnp.dot(a, w_ref[0], preferred_element_type=F32, precision=HIGHEST) + b_ref[0]


def _ada_call(c_all, w_ada, b_ada):
    depth, d, n = w_ada.shape
    rows = c_all.shape[0]
    tn = 512
    return pl.pallas_call(
        _ada_kernel,
        grid=(depth, n // tn),
        in_specs=[
            pl.BlockSpec((rows, d), lambda l, j: (0, 0)),
            pl.BlockSpec((1, d, tn), lambda l, j: (l, 0, j)),
            pl.BlockSpec((1, 1, tn), lambda l, j: (l, 0, j)),
        ],
        out_specs=pl.BlockSpec((1, rows, tn), lambda l, j: (l, 0, j)),
        out_shape=jax.ShapeDtypeStruct((depth, rows, n), F32),
        compiler_params=_cparams("arbitrary", "arbitrary"),
        name="ada_mod",
    )(c_all, w_ada, b_ada.reshape(depth, 1, n))


def _mod_spec(mod, tm):
    d = mod.shape[1]
    if mod.shape[0] == 1:
        return pl.BlockSpec((1, d), lambda i, *_: (0, 0))
    return pl.BlockSpec((tm, d), lambda i, *_: (i, 0))


def _norm_matmul_kernel(x_ref, g_ref, sh_ref, sc_ref, w_ref, o_ref):
    h = _rms_mod(x_ref[...], g_ref[...], sh_ref[...], sc_ref[...]).astype(BF16)
    o_ref[...] = jnp.dot(h, w_ref[...], preferred_element_type=F32)


def _norm_matmul(x, g, shift, scale, w, tm):
    m, d = x.shape
    tm = min(tm, m)
    n = w.shape[1]
    return pl.pallas_call(
        _norm_matmul_kernel,
        grid=(m // tm,),
        in_specs=[
            pl.BlockSpec((tm, d), lambda i: (i, 0)),
            pl.BlockSpec((1, d), lambda i: (0, 0)),
            _mod_spec(shift, tm),
            _mod_spec(scale, tm),
            pl.BlockSpec((d, n), lambda i: (0, 0)),
        ],
        out_specs=pl.BlockSpec((tm, n), lambda i: (i, 0)),
        out_shape=jax.ShapeDtypeStruct((m, n), F32),
        compiler_params=_cparams("arbitrary"),
        name="norm_matmul",
    )(x, g, shift, scale, w)


def _mm_res_kernel(a_ref, w_ref, x_ref, gate_ref, o_ref):
    y = jnp.dot(a_ref[...], w_ref[...], preferred_element_type=F32)
    o_ref[...] = x_ref[...] + gate_ref[...] * y


def _mm_res(a, w, x, gate, tm):
    m, k = a.shape
    tm = min(tm, m)
    n = w.shape[1]
    return pl.pallas_call(
        _mm_res_kernel,
        grid=(m // tm,),
        in_specs=[
            pl.BlockSpec((tm, k), lambda i: (i, 0)),
            pl.BlockSpec((k, n), lambda i: (0, 0)),
            pl.BlockSpec((tm, n), lambda i: (i, 0)),
            _mod_spec(gate, tm),
        ],
        out_specs=pl.BlockSpec((tm, n), lambda i: (i, 0)),
        out_shape=jax.ShapeDtypeStruct((m, n), F32),
        compiler_params=_cparams("arbitrary"),
        name="matmul_residual",
    )(a, w, x, gate)


def _bmm_kernel(a_ref, b_ref, o_ref):
    o_ref[0] = jnp.dot(a_ref[0], b_ref[0], preferred_element_type=F32, precision=HIGHEST)


def _bmm(a, b):
    n, m, k = a.shape
    p = b.shape[2]
    return pl.pallas_call(
        _bmm_kernel,
        grid=(n,),
        in_specs=[pl.BlockSpec((1, m, k), lambda i: (i, 0, 0)), pl.BlockSpec((1, k, p), lambda i: (i, 0, 0))],
        out_specs=pl.BlockSpec((1, m, p), lambda i: (i, 0, 0)),
        out_shape=jax.ShapeDtypeStruct((n, m, p), F32),
        compiler_params=_cparams("arbitrary"),
        name="fold_weights",
    )(a, b)


_QW = MOBA_HEADS * LANES
_KW = MOBA_KV_HEADS * LANES


def _moba_qkv_kernel(x_ref, g_ref, sh_ref, sc_ref, w_ref, q_ref, ka_ref, vp_ref, ko_ref, vo_ref, km_ref):
    i = pl.program_id(0)
    tm = x_ref.shape[0]
    half = LANES // 2

    @pl.when(i == 0)
    def _():
        km_ref[...] = jnp.zeros_like(km_ref)

    h = _rms_mod(x_ref[...], g_ref[...], sh_ref[...], sc_ref[...]).astype(BF16)
    y = jnp.dot(h, w_ref[...], preferred_element_type=F32)
    lane = lax.broadcasted_iota(jnp.int32, (tm, LANES), 1)
    lanef = lane.astype(F32)
    kvw = MOBA_KV_HEADS * MOBA_HEAD_DIM
    ko_ref[...] = y[:, _QW + 2 * _KW:_QW + 2 * _KW + kvw]
    vo_ref[...] = y[:, _QW + 2 * _KW + kvw:_QW + 2 * _KW + 2 * kvw]
    for k in range(MOBA_KV_HEADS):
        kp = y[:, _QW + k * LANES:_QW + (k + 1) * LANES]
        km_ref[k, pl.ds(half + i, 1), :] = jnp.mean(kp, axis=0, keepdims=True)
        ka_ref[:, k * LANES:(k + 1) * LANES] = jnp.where(lane == half + i, 1.0, kp).astype(BF16)
        vp = y[:, _QW + _KW + k * LANES:_QW + _KW + (k + 1) * LANES]
        vp_ref[:, k * LANES:(k + 1) * LANES] = jnp.where(lane == LANES - 1, 1.0, vp).astype(BF16)
    past = (lane >= half) & (lane < half + i)
    own_bias = jnp.where(lane == half + i, 0.0, NEG)
    for hd in range(MOBA_HEADS):
        qp = y[:, hd * LANES:(hd + 1) * LANES]
        gs = _nt_dot(qp, km_ref[hd // MOBA_GROUP], precision=HIGHEST)
        bias = _top_mask_bias(jnp.where(past, gs, -jnp.inf), lanef, MOBA_TOPK, own_bias)
        q_ref[hd] = jnp.where(lane < half, qp * (MOBA_HEAD_DIM ** -0.5 * LOG2E), bias).astype(BF16)


def _moba_qkv_call(x, g, shift, scale, w):
    t, d = x.shape
    tm = MOBA_BLOCK
    assert t // tm <= LANES // 2
    n = w.shape[1]
    kvw = MOBA_KV_HEADS * MOBA_HEAD_DIM
    return pl.pallas_call(
        _moba_qkv_kernel,
        grid=(t // tm,),
        in_specs=[
            pl.BlockSpec((tm, d), lambda i: (i, 0)),
            pl.BlockSpec((1, d), lambda i: (0, 0)),
            pl.BlockSpec((1, d), lambda i: (0, 0)),
            pl.BlockSpec((1, d), lambda i: (0, 0)),
            pl.BlockSpec((d, n), lambda i: (0, 0)),
        ],
        out_specs=[
            pl.BlockSpec((MOBA_HEADS, tm, LANES), lambda i: (0, i, 0)),
            pl.BlockSpec((tm, _KW), lambda i: (i, 0)),
            pl.BlockSpec((tm, _KW), lambda i: (i, 0)),
            pl.BlockSpec((tm, kvw), lambda i: (i, 0)),
            pl.BlockSpec((tm, kvw), lambda i: (i, 0)),
        ],
        out_shape=[
            jax.ShapeDtypeStruct((MOBA_HEADS, t, LANES), BF16),
            jax.ShapeDtypeStruct((t, _KW), BF16),
            jax.ShapeDtypeStruct((t, _KW), BF16),
            jax.ShapeDtypeStruct((t, kvw), F32),
            jax.ShapeDtypeStruct((t, kvw), F32),
        ],
        scratch_shapes=[pltpu.VMEM((MOBA_KV_HEADS, LANES, LANES), F32)],
        compiler_params=_cparams("arbitrary"),
        name="moba_qkv_gate",
    )(x, g, shift, scale, w)


def _moba_attn_kernel(q_ref, k_ref, v_ref, al_ref, sl_ref, o_ref, m_sc, acc_sc):
    i = pl.program_id(1)
    tq = q_ref.shape[1]
    blk = MOBA_BLOCK
    n_chains, rows, _ = m_sc.shape
    hpc = rows // tq
    m_sc[...] = jnp.full_like(m_sc, NEG)
    acc_sc[...] = jnp.zeros_like(acc_sc)

    def step(j, nblk, masked):
        tk = nblk * blk
        start = pl.multiple_of(j * blk, blk)
        dj = ((j - i) * blk).astype(F32)
        for u in range(n_chains):
            kv = u * hpc // MOBA_GROUP
            kb = k_ref[pl.ds(start, tk), kv * LANES:(kv + 1) * LANES]
            vb = v_ref[pl.ds(start, tk), kv * LANES:(kv + 1) * LANES]
            q = q_ref[u * hpc:(u + 1) * hpc].reshape(rows, LANES)
            s = _nt_dot(q, kb) + al_ref[u, :, :tk]
            if masked:
                row_pos = (nblk - 1) * blk + (lax.broadcasted_iota(jnp.int32, (rows, tk), 0) & (tq - 1))
                col = lax.broadcasted_iota(jnp.int32, (rows, tk), 1)
                s = jnp.where(col <= row_pos, s, NEG)
            c = sl_ref[u] * dj
            m_prev = m_sc[u]
            m_new = jnp.maximum(m_prev, jnp.max(s, axis=-1, keepdims=True) + c)
            shift = m_new - c
            p = jnp.exp2(s - jnp.concatenate([shift] * (tk // LANES), axis=1))
            acc_sc[u] = acc_sc[u] * jnp.exp2(m_prev - m_new) + jnp.dot(p.astype(BF16), vb, preferred_element_type=F32)
            m_sc[u] = m_new

    def past(jj, _):
        step(2 * jj, 2, False)
        return 0

    lax.fori_loop(0, i // 2, past, 0)

    @pl.when(i % 2 == 1)
    def _():
        step(i - 1, 2, True)

    @pl.when(i % 2 == 0)
    def _():
        step(i, 1, True)
    for u in range(n_chains):
        acc = acc_sc[u]
        o = acc / acc[:, LANES - 1:]
        for x in range(hpc):
            hd = u * hpc + x
            o_ref[:, hd * LANES:(hd + 1) * LANES] = o[x * tq:(x + 1) * tq].astype(BF16)


def _moba_attn_call(q_aug, k_aug, v_pad, al_tab, sl_tab):
    nh, t, _ = q_aug.shape
    tq = MOBA_BLOCK
    assert tq & (tq - 1) == 0
    kvps = 2
    n_steps = MOBA_KV_HEADS // kvps
    n_chains, rows = al_tab.shape[0] // n_steps, al_tab.shape[1]
    gw = kvps * MOBA_GROUP * LANES
    once = pl.Buffered(1)
    return pl.pallas_call(
        _moba_attn_kernel,
        grid=(n_steps, t // tq),
        in_specs=[
            pl.BlockSpec((kvps * MOBA_GROUP, tq, LANES), lambda k, i: (k, i, 0)),
            pl.BlockSpec((t, kvps * LANES), lambda k, i: (0, k), pipeline_mode=once),
            pl.BlockSpec((t, kvps * LANES), lambda k, i: (0, k), pipeline_mode=once),
            pl.BlockSpec((n_chains, rows, 2 * MOBA_BLOCK), lambda k, i: (k, 0, 0), pipeline_mode=once),
            pl.BlockSpec((n_chains, rows, LANES), lambda k, i: (k, 0, 0), pipeline_mode=once),
        ],
        out_specs=pl.BlockSpec((tq, gw), lambda k, i: (i, k)),
        out_shape=jax.ShapeDtypeStruct((t, nh * LANES), BF16),
        scratch_shapes=[pltpu.VMEM((n_chains, rows, LANES), F32), pltpu.VMEM((n_chains, rows, LANES), F32)],
        compiler_params=_cparams("arbitrary", "arbitrary"),
        name="moba_attn",
    )(q_aug, k_aug, v_pad, al_tab, sl_tab)


def _moe_kernel(x_ref, g_ref, sh_ref, sc_ref, gt_ref, gf_ref, wg_ref, we_ref, wgu_ref, wdn_ref, o_ref,
                h_sc, gate_sc, acc_sc, *, final):
    e = pl.program_id(1)
    tm = x_ref.shape[0]
    lane = lax.broadcasted_iota(jnp.int32, (tm, LANES), 1)

    @pl.when(e == 0)
    def _():
        h = _rms_mod(x_ref[...], g_ref[...], sh_ref[...], sc_ref[...])
        h_sc[...] = h.astype(BF16)
        lanef = lane.astype(F32)
        gl = jnp.dot(h, wg_ref[...], preferred_element_type=F32, precision=HIGHEST)
        gl = jnp.where(lane < N_GROUPS, gl, -jnp.inf)
        gmax = jnp.max(gl, axis=-1, keepdims=True)
        g_sel = jnp.min(jnp.where(gl == gmax, lanef, 1e9), axis=-1, keepdims=True)
        g_p = 1.0 / jnp.sum(jnp.exp(gl - gmax), axis=-1, keepdims=True)
        el = jnp.dot(h, we_ref[...], preferred_element_type=F32, precision=HIGHEST)
        lo = g_sel * EXPERTS_PER_GROUP
        el = jnp.where((lanef >= lo) & (lanef < lo + EXPERTS_PER_GROUP), el, -jnp.inf)
        emax = jnp.max(el, axis=-1, keepdims=True)
        ex = jnp.exp(el - emax)
        prob = ex / jnp.sum(ex, axis=-1, keepdims=True)
        prob = jnp.where(el > -jnp.inf, prob, -jnp.inf)
        p1 = jnp.max(prob, axis=-1, keepdims=True)
        i1 = jnp.min(jnp.where(prob == p1, lanef, 1e9), axis=-1, keepdims=True)
        prob2 = jnp.where(lanef == i1, -jnp.inf, prob)
        p2 = jnp.max(prob2, axis=-1, keepdims=True)
        i2 = jnp.min(jnp.where(prob2 == p2, lanef, 1e9), axis=-1, keepdims=True)
        den = p1 + p2
        gate_sc[...] = jnp.where(lanef == i1, p1 / den * g_p, jnp.where(lanef == i2, p2 / den * g_p, 0.0))
        acc_sc[...] = jnp.zeros_like(acc_sc)

    gcol = jnp.sum(jnp.where(lane == e, gate_sc[...], 0.0), axis=-1, keepdims=True)
    gu = jnp.dot(h_sc[...], wgu_ref[0], preferred_element_type=F32)
    gg = gu[:, :D_EXPERT]
    a = (gg * jax.nn.sigmoid(gg)) * gu[:, D_EXPERT:] * gcol
    acc_sc[...] += jnp.dot(a.astype(BF16), wdn_ref[0], preferred_element_type=F32)

    @pl.when(e == pl.num_programs(1) - 1)
    def _():
        y = x_ref[...] + gt_ref[...] * acc_sc[...]
        if final:
            y = y * lax.rsqrt(jnp.mean(y * y, axis=-1, keepdims=True) + NORM_EPS) * gf_ref[...]
        o_ref[...] = y


def _moe_call(x, g, shift, scale, gate, g_final, w_group, w_expert, w_gu, w_down, tm, final):
    m, d = x.shape
    tm = min(tm, m)
    ne, _, f2 = w_gu.shape
    fd = w_down.shape[1]
    return pl.pallas_call(
        functools.partial(_moe_kernel, final=final),
        grid=(m // tm, ne),
        in_specs=[
            pl.BlockSpec((tm, d), lambda i, e: (i, 0)),
            pl.BlockSpec((1, d), lambda i, e: (0, 0)),
            _mod_spec(shift, tm),
            _mod_spec(scale, tm),
            _mod_spec(gate, tm),
            pl.BlockSpec((1, d), lambda i, e: (0, 0)),
            pl.BlockSpec((d, LANES), lambda i, e: (0, 0)),
            pl.BlockSpec((d, LANES), lambda i, e: (0, 0)),
            pl.BlockSpec((1, d, f2), lambda i, e: (e, 0, 0)),
            pl.BlockSpec((1, fd, d), lambda i, e: (e, 0, 0)),
        ],
        out_specs=pl.BlockSpec((tm, d), lambda i, e: (i, 0)),
        out_shape=jax.ShapeDtypeStruct((m, d), F32),
        scratch_shapes=[pltpu.VMEM((tm, d), BF16), pltpu.VMEM((tm, LANES), F32), pltpu.VMEM((tm, d), F32)],
        compiler_params=_cparams("arbitrary", "arbitrary"),
        name="hier_moe",
    )(x, g, shift, scale, gate, g_final, w_group, w_expert, w_gu, w_down)


_MLA_W1 = MLA_Q_LORA + MLA_KV_LORA + 2 * LANES


def _mla_qkv_kernel(x_ref, g_ref, sh_ref, sc_ref, cos_ref, sin_ref, w1_ref, gq_ref, gkv_ref, wl_ref, wr_ref, wrr_ref,
                    q_ref, ckv_ref, kr_ref, kf_ref):
    h = _rms_mod(x_ref[...], g_ref[...], sh_ref[...], sc_ref[...]).astype(BF16)
    a = jnp.dot(h, w1_ref[...], preferred_element_type=F32)
    cos = cos_ref[...]
    sin = sin_ref[...]
    cq = a[:, :MLA_Q_LORA]
    cqn = (cq * lax.rsqrt(jnp.mean(cq * cq, axis=-1, keepdims=True) + NORM_EPS) * gq_ref[...]).astype(BF16)
    kvc = a[:, MLA_Q_LORA:MLA_Q_LORA + MLA_KV_LORA]
    ckv = kvc * lax.rsqrt(jnp.mean(kvc * kvc, axis=-1, keepdims=True) + NORM_EPS) * gkv_ref[...]
    o = MLA_Q_LORA + MLA_KV_LORA
    kr = a[:, o:o + LANES] * cos + a[:, o + LANES:o + 2 * LANES] * sin
    ckv_ref[...] = ckv
    kr_ref[...] = kr[:, :MLA_ROPE]
    kf_ref[:, :LANES] = ckv.astype(BF16)
    lane = lax.broadcasted_iota(jnp.int32, kr.shape, 1)
    kf_ref[:, LANES:] = jnp.where(lane == LANES - 1, 1.0, kr).astype(BF16)
    scale = (MLA_NOPE + MLA_ROPE) ** -0.5 * LOG2E
    ql = jnp.dot(cqn, wl_ref[...], preferred_element_type=F32)
    qr = jnp.dot(cqn, wr_ref[...], preferred_element_type=F32)
    qrr = jnp.dot(cqn, wrr_ref[...], preferred_element_type=F32)
    for hd in range(MLA_HEADS):
        sl = slice(hd * LANES, (hd + 1) * LANES)
        q_ref[hd, :, :LANES] = (ql[:, sl] * scale).astype(BF16)
        q_ref[hd, :, LANES:] = ((qr[:, sl] * cos + qrr[:, sl] * sin) * scale).astype(BF16)


def _mla_qkv_call(x, g, shift, scale, cos, sin, w1, gq, gkv, wl, wr, wrr, tm):
    m, d = x.shape
    tm = min(tm, m)
    hw = MLA_HEADS * LANES
    full = lambda shape: pl.BlockSpec(shape, lambda i: (0,) * len(shape))
    return pl.pallas_call(
        _mla_qkv_kernel,
        grid=(m // tm,),
        in_specs=[
            pl.BlockSpec((tm, d), lambda i: (i, 0)),
            full((1, d)),
            _mod_spec(shift, tm),
            _mod_spec(scale, tm),
            pl.BlockSpec((tm, LANES), lambda i: (i, 0)),
            pl.BlockSpec((tm, LANES), lambda i: (i, 0)),
            full((d, _MLA_W1)),
            full((1, MLA_Q_LORA)),
            full((1, MLA_KV_LORA)),
            full((MLA_Q_LORA, hw)),
            full((MLA_Q_LORA, hw)),
            full((MLA_Q_LORA, hw)),
        ],
        out_specs=[
            pl.BlockSpec((MLA_HEADS, tm, 2 * LANES), lambda i: (0, i, 0)),
            pl.BlockSpec((tm, MLA_KV_LORA), lambda i: (i, 0)),
            pl.BlockSpec((tm, MLA_ROPE), lambda i: (i, 0)),
            pl.BlockSpec((tm, 2 * LANES), lambda i: (i, 0)),
        ],
        out_shape=[
            jax.ShapeDtypeStruct((MLA_HEADS, m, 2 * LANES), BF16),
            jax.ShapeDtypeStruct((m, MLA_KV_LORA), F32),
            jax.ShapeDtypeStruct((m, MLA_ROPE), F32),
            jax.ShapeDtypeStruct((m, 2 * LANES), BF16),
        ],
        compiler_params=_cparams("arbitrary"),
        name="mla_qkv",
    )(x, g, shift, scale, cos, sin, w1, gq, gkv, wl, wr, wrr)


def _mla_attn_kernel(q_ref, k_ref, o_ref, m_sc, acc_sc, *, tk, hpc, cpb):
    i = pl.program_id(0)
    nh, tq, w = q_ref.shape
    rows = hpc * tq
    n_chains = nh // hpc
    n_past = (i * tq) // tk
    m_sc[...] = jnp.full_like(m_sc, NEG)
    acc_sc[...] = jnp.zeros_like(acc_sc)

    def tile(j, masked):
        start = pl.multiple_of(j * tk, tk)
        kb = k_ref[pl.ds(start, tk), :]

        def chain_pair(cp, _):
            for u in range(cpb):
                c = cp * cpb + u
                q = q_ref[pl.ds(c * hpc, hpc)].reshape(rows, w)
                s = _nt_dot(q, kb)
                if masked:
                    row_pos = i * tq + (lax.broadcasted_iota(jnp.int32, (rows, tk), 0) & (tq - 1))
                    col_pos = start + lax.broadcasted_iota(jnp.int32, (rows, tk), 1)
                    s = jnp.where(col_pos <= row_pos, s, NEG)
                m_prev = m_sc[c]
                m_new = jnp.maximum(m_prev, jnp.max(s, axis=-1, keepdims=True))
                p = jnp.exp2(s - jnp.concatenate([m_new] * (tk // LANES), axis=1))
                alpha = jnp.exp2(m_prev - m_new)
                acc_sc[c] = (acc_sc[c] * jnp.concatenate([alpha] * (w // LANES), axis=1)
                             + jnp.dot(p.astype(BF16), kb, preferred_element_type=F32))
                m_sc[c] = m_new
            return 0

        lax.fori_loop(0, n_chains // cpb, chain_pair, 0)

    def past(j, _):
        tile(j, False)
        return 0

    lax.fori_loop(0, n_past, past, 0)
    tile(n_past, True)
    for c in range(n_chains):
        acc = acc_sc[c]
        o = acc[:, :LANES] / acc[:, w - 1:]
        for x in range(hpc):
            hd = c * hpc + x
            o_ref[:, hd * LANES:(hd + 1) * LANES] = o[x * tq:(x + 1) * tq].astype(BF16)


def _mla_attn_call(q_full, k_full, tq, tk, hpc, cpb):
    nh, t, w = q_full.shape
    assert tq & (tq - 1) == 0 and tk % tq == 0 and t % tk == 0 and nh % (cpb * hpc) == 0
    n_chains, rows = nh // hpc, hpc * tq
    return pl.pallas_call(
        functools.partial(_mla_attn_kernel, tk=tk, hpc=hpc, cpb=cpb),
        grid=(t // tq,),
        in_specs=[
            pl.BlockSpec((nh, tq, w), lambda i: (0, i, 0)),
            pl.BlockSpec((t, w), lambda i: (0, 0)),
        ],
        out_specs=pl.BlockSpec((tq, nh * LANES), lambda i: (i, 0)),
        out_shape=jax.ShapeDtypeStruct((t, nh * LANES), BF16),
        scratch_shapes=[pltpu.VMEM((n_chains, rows, LANES), F32), pltpu.VMEM((n_chains, rows, w), F32)],
        compiler_params=_cparams("arbitrary"),
        name="mla_attn",
    )(q_full, k_full)


def _moba_sample_kernel(pt_ref, q_ref, slope_ref, tpos_ref, hk_ref, kn_ref, vn_ref, k_hbm, v_hbm, o_ref,
                        kbuf, vbuf, ksem, vsem, s_sc, p_sc, g_sc, b_sc, pn_sc, l_sc, *, n_pages, past_len):
    pps = MOBA_PAGES_PER_STEP
    seq = pl.program_id(0)
    n_seq = pl.num_programs(0)
    n_chunks = n_pages // pps
    n_blocks = n_pages // 2
    q = q_ref[0]
    rows = q.shape[0]
    lane = lax.broadcasted_iota(jnp.int32, (rows, LANES), 1)
    lanef = lane.astype(F32)

    def k_copies(sq, c):
        return [pltpu.make_async_copy(k_hbm.at[pt_ref[sq, c * pps + p]], kbuf.at[c % 2, p], ksem.at[c % 2])
                for p in range(pps)]

    def v_copies(sq, c):
        return [pltpu.make_async_copy(v_hbm.at[pt_ref[sq, c * pps + p]], vbuf.at[c, p], vsem.at[c])
                for p in range(pps)]

    def start(copies):
        for cp in copies:
            cp.start()

    def wait(copies):
        for cp in copies:
            cp.wait()

    n_ahead = min(2, n_chunks)

    @pl.when(seq == 0)
    def _():
        for c in range(n_ahead):
            start(k_copies(0, c))

    for c in range(n_chunks):
        slot = c % 2
        wait(k_copies(seq, c))
        for a in range(pps // 2):
            kb = jnp.concatenate([kbuf[slot, 2 * a].astype(BF16), kbuf[slot, 2 * a + 1].astype(BF16)], axis=1)
            s2 = jnp.dot(q, kb, preferred_element_type=F32)
            s_sc[c * pps + 2 * a] = s2[:, :LANES]
            s_sc[c * pps + 2 * a + 1] = s2[:, LANES:]
        if c + 2 < n_chunks:
            start(k_copies(seq, c + 2))
        if c == max(n_chunks - 3, 0):
            for cv in range(n_chunks):
                start(v_copies(seq, cv))

    def softmax():
        for b in range(n_blocks):
            r = jnp.sum(s_sc[2 * b] + s_sc[2 * b + 1], axis=-1, keepdims=True)
            g_sc[b] = jnp.broadcast_to(r, (rows, LANES))
            b_sc[b] = jnp.full((rows, LANES), NEG, F32)
        for _ in range(min(MOBA_TOPK, n_blocks)):
            best = g_sc[0]
            for b in range(1, n_blocks):
                best = jnp.maximum(best, g_sc[b])
            idx = jnp.full((rows, LANES), float(n_blocks), F32)
            for b in reversed(range(n_blocks)):
                idx = jnp.where(g_sc[b] == best, float(b), idx)
            for b in range(n_blocks):
                hit = idx == float(b)
                b_sc[b] = jnp.where(hit, 0.0, b_sc[b])
                g_sc[b] = jnp.where(hit, -jnp.inf, g_sc[b])
        slope = jnp.broadcast_to(slope_ref[...], (rows, LANES))
        tpos = jnp.broadcast_to(tpos_ref[...], (rows, LANES))
        al_lane = slope * lanef
        sn = _nt_dot(q, kn_ref[0]) - slope * (tpos - (past_len + lanef))
        sn = jnp.where(past_len + lanef <= tpos, sn, NEG)
        mv = sn
        for b in range(n_blocks):
            for u in range(2):
                pg = 2 * b + u
                sc = s_sc[pg] + al_lane + (b_sc[b] + slope * (float(pg * LANES) - tpos))
                s_sc[pg] = sc
                mv = jnp.maximum(mv, sc)
        m = jnp.broadcast_to(jnp.max(mv, axis=-1, keepdims=True), (rows, LANES))
        pn = jnp.exp(sn - m)
        lv = pn
        for pg in range(n_pages):
            p = jnp.exp(s_sc[pg] - m)
            p_sc[pg] = p.astype(BF16)
            lv = lv + p
        l_sc[...] = jnp.sum(lv, axis=-1, keepdims=True)
        pn_sc[...] = pn.astype(BF16)

    softmax()
    @pl.when(seq + 1 < n_seq)
    def _():
        for c in range(n_ahead):
            start(k_copies(seq + 1, c))

    acc = jnp.dot(pn_sc[...], vn_ref[0], preferred_element_type=F32)
    for c in range(n_chunks):
        wait(v_copies(seq, c))
        for a in range(pps // 2):
            pg = c * pps + 2 * a
            p2 = jnp.concatenate([p_sc[pg], p_sc[pg + 1]], axis=1)
            vt = jnp.concatenate([vbuf[c, 2 * a].astype(BF16), vbuf[c, 2 * a + 1].astype(BF16)], axis=1)
            acc += _nt_dot(p2, vt)
    o_ref[0] = (acc / l_sc[...] * hk_ref[...]).astype(BF16)


def _moba_sample_call(page_table, qbd, slope_col, tpos_col, hk_mask, k_new, v_new, cache_k, cache_v):
    nb, n_pages = page_table.shape
    rows, w = qbd.shape[1:]
    page = cache_k.shape[2]
    pps = MOBA_PAGES_PER_STEP
    assert page == LANES and n_pages % pps == 0 and pps % 2 == 0

    per_seq = lambda shape: pl.BlockSpec((1,) + shape, lambda b, pt: (b, 0, 0))
    const = lambda shape: pl.BlockSpec(shape, lambda b, pt: (0, 0))
    hbm = pl.BlockSpec(memory_space=pl.ANY)
    grid_spec = pltpu.PrefetchScalarGridSpec(
        num_scalar_prefetch=1,
        grid=(nb,),
        in_specs=[per_seq((rows, w)), const((rows, 1)), const((rows, 1)), const((rows, w)),
                  per_seq((LANES, w)), per_seq((LANES, w)), hbm, hbm],
        out_specs=per_seq((rows, w)),
        scratch_shapes=[
            pltpu.VMEM((2, pps, w, page), F32),
            pltpu.VMEM((n_pages // pps, pps, w, page), F32),
            pltpu.SemaphoreType.DMA((2,)),
            pltpu.SemaphoreType.DMA((n_pages // pps,)),
            pltpu.VMEM((n_pages, rows, LANES), F32),
            pltpu.VMEM((n_pages, rows, LANES), BF16),
            pltpu.VMEM((n_pages // 2, rows, LANES), F32),
            pltpu.VMEM((n_pages // 2, rows, LANES), F32),
            pltpu.VMEM((rows, LANES), BF16),
            pltpu.VMEM((rows, 1), F32),
        ],
    )
    return pl.pallas_call(
        functools.partial(_moba_sample_kernel, n_pages=n_pages, past_len=n_pages * page),
        grid_spec=grid_spec,
        out_shape=jax.ShapeDtypeStruct((nb, rows, w), BF16),
        compiler_params=_cparams("arbitrary"),
        name="moba_sample_attn",
    )(page_table, qbd, slope_col, tpos_col, hk_mask, k_new, v_new, cache_k, cache_v)


def _mla_sample_kernel(pt_ref, q_ref, trow_ref, kn_ref, c_hbm, r_hbm, o_ref, cbuf, rbuf, sem, cb_sc, s_sc, *, n_pages):
    cpp = PAGES_PER_STEP
    n_chunks = n_pages // cpp
    kw = cpp * LANES
    seq = pl.program_id(0)
    n_seq = pl.num_programs(0)
    slot = seq % 2
    q = q_ref[0]
    ql = q[:, :LANES]
    qr = q[:, LANES:LANES + MLA_ROPE]
    rows = q.shape[0]

    def seq_copies(sq, sl):
        out = []
        for p in range(n_pages):
            pid = pt_ref[sq, p]
            out.append(pltpu.make_async_copy(c_hbm.at[pid], cbuf.at[sl, p], sem.at[0, sl]))
            out.append(pltpu.make_async_copy(r_hbm.at[pid], rbuf.at[sl, p], sem.at[1, sl]))
        return out

    @pl.when(seq == 0)
    def _():
        for cp in seq_copies(0, 0):
            cp.start()

    for cp in seq_copies(seq, slot):
        cp.wait()

    @pl.when(seq + 1 < n_seq)
    def _():
        for cp in seq_copies(seq + 1, 1 - slot):
            cp.start()

    mv = jnp.full((rows, kw), NEG, F32)
    for c in range(n_chunks):
        cb = cbuf[slot, c * cpp:(c + 1) * cpp].reshape(kw, MLA_KV_LORA).astype(BF16)
        cb_sc[c] = cb
        rb = jnp.concatenate([rbuf[slot, c * cpp + p].astype(BF16) for p in range(cpp)], axis=1)
        s = _nt_dot(ql, cb) + jnp.dot(qr, rb, preferred_element_type=F32)
        s_sc[c] = s
        mv = jnp.maximum(mv, s)
    kn = kn_ref[0]
    lanef = lax.broadcasted_iota(jnp.int32, (rows, LANES), 1).astype(F32)
    sn = jnp.where(lanef <= trow_ref[...], _nt_dot(q, kn), NEG)
    m = jnp.maximum(jnp.max(mv, axis=-1, keepdims=True), jnp.max(sn, axis=-1, keepdims=True))
    pn = jnp.exp2(sn - m)
    acc = jnp.dot(pn.astype(BF16), kn[:, :LANES], preferred_element_type=F32)
    lv = jnp.zeros((rows, kw), F32)
    for c in range(n_chunks):
        pr = jnp.exp2(s_sc[c] - m)
        lv = lv + pr
        acc = acc + jnp.dot(pr.astype(BF16), cb_sc[c], preferred_element_type=F32)
    l = jnp.sum(lv, axis=-1, keepdims=True) + jnp.sum(pn, axis=-1, keepdims=True)
    o_ref[0] = (acc / l).astype(BF16)


def _mla_sample_call(page_table, q_rows, trow_col, k_new, cache_ckv, cache_krope):
    nb, n_pages = page_table.shape
    rows, w = q_rows.shape[1:]
    page = cache_ckv.shape[1]
    cpp = PAGES_PER_STEP
    assert page == LANES and n_pages % cpp == 0
    n_chunks = n_pages // cpp

    per_seq = lambda shape: pl.BlockSpec((1,) + shape, lambda b, pt: (b, 0, 0))
    hbm = pl.BlockSpec(memory_space=pl.ANY)
    grid_spec = pltpu.PrefetchScalarGridSpec(
        num_scalar_prefetch=1,
        grid=(nb,),
        in_specs=[per_seq((rows, w)), pl.BlockSpec((rows, 1), lambda b, pt: (0, 0)), per_seq((LANES, w)), hbm, hbm],
        out_specs=per_seq((rows, LANES)),
        scratch_shapes=[
            pltpu.VMEM((2, n_pages, page, MLA_KV_LORA), F32),
            pltpu.VMEM((2, n_pages, MLA_ROPE, page), F32),
            pltpu.SemaphoreType.DMA((2, 2)),
            pltpu.VMEM((n_chunks, cpp * page, MLA_KV_LORA), BF16),
            pltpu.VMEM((n_chunks, rows, cpp * page), F32),
        ],
    )
    return pl.pallas_call(
        functools.partial(_mla_sample_kernel, n_pages=n_pages),
        grid_spec=grid_spec,
        out_shape=jax.ShapeDtypeStruct((nb, rows, LANES), BF16),
        compiler_params=_cparams("arbitrary"),
        name="mla_sample_attn",
    )(page_table, q_rows, trow_col, k_new, cache_ckv, cache_krope)


def _pad_last(w, width):
    return jnp.pad(w, [(0, 0)] * (w.ndim - 1) + [(0, width - w.shape[-1])])


def _moba_weights(w_qkv, w_o):
    d = w_qkv.shape[0]
    nq = MOBA_HEADS * MOBA_HEAD_DIM
    nk = MOBA_KV_HEADS * MOBA_HEAD_DIM
    wq = _pad_last(w_qkv[:, :nq].reshape(d, MOBA_HEADS, MOBA_HEAD_DIM), LANES).reshape(d, _QW)
    wk = w_qkv[:, nq:nq + nk]
    wv = w_qkv[:, nq + nk:]
    wkp = _pad_last(wk.reshape(d, MOBA_KV_HEADS, MOBA_HEAD_DIM), LANES).reshape(d, _KW)
    wvp = _pad_last(wv.reshape(d, MOBA_KV_HEADS, MOBA_HEAD_DIM), LANES).reshape(d, _KW)
    w_fused = jnp.concatenate([wq, wkp, wvp, wk, wv], axis=1).astype(BF16)
    wo = w_o.reshape(MOBA_HEADS, MOBA_HEAD_DIM, d)
    wo_prompt = jnp.pad(wo, ((0, 0), (0, LANES - MOBA_HEAD_DIM), (0, 0))).reshape(_QW, d).astype(BF16)
    kv_of = jnp.arange(MOBA_HEADS) // MOBA_GROUP
    onehot = (kv_of[:, None] == jnp.arange(MOBA_KV_HEADS)[None, :]).astype(F32)
    wo_sample = (onehot[:, :, None, None] * wo[:, None, :, :]).reshape(MOBA_HEADS * nk, d).astype(BF16)
    return w_fused, wo_prompt, wo_sample


def _rotate_half_cols(w):
    half = w.shape[-1] // 2
    return jnp.concatenate([-w[..., half:], w[..., :half]], axis=-1)


def _mla_weights(w_dq, w_uq, w_dkv, w_uk, w_uv, w_o):
    d = w_dq.shape[0]
    w_r = w_dkv[:, MLA_KV_LORA:]
    w1 = jnp.concatenate([w_dq, w_dkv[:, :MLA_KV_LORA], _pad_last(w_r, LANES), _pad_last(_rotate_half_cols(w_r), LANES)],
                         axis=1).astype(BF16)
    wq = w_uq.reshape(MLA_Q_LORA, MLA_HEADS, MLA_NOPE + MLA_ROPE)
    wq_nope = wq[:, :, :MLA_NOPE].transpose(1, 0, 2)
    wuk_t = w_uk.transpose(1, 2, 0)
    w_lat = _bmm(wq_nope, wuk_t).transpose(1, 0, 2).reshape(MLA_Q_LORA, MLA_HEADS * LANES).astype(BF16)
    wq_rope = wq[:, :, MLA_NOPE:]
    w_qr = _pad_last(wq_rope, LANES).reshape(MLA_Q_LORA, MLA_HEADS * LANES).astype(BF16)
    w_qrr = _pad_last(_rotate_half_cols(wq_rope), LANES).reshape(MLA_Q_LORA, MLA_HEADS * LANES).astype(BF16)
    wuv = w_uv.transpose(1, 0, 2)
    wo = w_o.reshape(MLA_HEADS, MLA_V, d)
    w_vo = _bmm(wuv, wo).reshape(MLA_HEADS * MLA_KV_LORA, d).astype(BF16)
    return w1, w_lat, w_qr, w_qrr, w_vo


def _rope_tables(pos):
    inv = ROPE_THETA ** (-jnp.arange(0, MLA_ROPE, 2, dtype=F32) / MLA_ROPE)
    ang = pos.astype(F32)[:, None] * inv[None, :]
    cos = jnp.cos(ang)
    sin = jnp.sin(ang)
    return _pad_last(jnp.concatenate([cos, cos], axis=-1), LANES), _pad_last(jnp.concatenate([sin, sin], axis=-1), LANES)


def kernel(x_prompt, x_sample, cache_moba_k, cache_moba_v, cache_mla_ckv, cache_mla_krope, page_table, c_prompt, c_sample, w_ada, b_ada, g_norm1, g_norm2, moba_w_qkv, moba_w_o, mla_w_dq, mla_g_q, mla_w_uq, mla_w_dkv, mla_g_kv, mla_w_uk, mla_w_uv, mla_w_o, moe_w_group, moe_w_expert, moe_w_gu, moe_w_down, g_final):
    bp, seq, d = x_prompt.shape
    assert bp == 1
    nb, tdec, _ = x_sample.shape
    n_pool, page = cache_moba_k.shape[:2]
    n_pages = page_table.shape[1]
    past = n_pages * page
    ms = nb * tdec
    xp = x_prompt.reshape(seq, d)
    xs = x_sample.reshape(ms, d)

    c_all = jnp.concatenate([c_sample, c_prompt, jnp.zeros((7, d), F32)], axis=0)
    ada = _ada_call(c_all, w_ada, b_ada).reshape(w_ada.shape[0], c_all.shape[0], N_ADA, d)

    def mods(layer):
        mp = [ada[layer, nb:nb + 1, i] for i in range(N_ADA)]
        msm = [jnp.repeat(ada[layer, :nb, i], tdec, axis=0) for i in range(N_ADA)]
        return mp, msm

    row = lambda v: v.reshape(1, -1)
    wg = [_pad_last(moe_w_group[l], LANES) for l in range(2)]
    we = [_pad_last(moe_w_expert[l], LANES) for l in range(2)]
    wgu = moe_w_gu.astype(BF16)
    wdn = moe_w_down.astype(BF16)
    gfin = row(g_final)

    def moe(x, layer, m, tm, final):
        return _moe_call(x, row(g_norm2[layer]), m[3], m[4], m[5], gfin, wg[layer], we[layer], wgu[layer], wdn[layer],
                         tm, final)

    mp, msm = mods(0)
    w_fused, wo_prompt, wo_sample = _moba_weights(moba_w_qkv, moba_w_o)
    g1 = row(g_norm1[0])
    slopes = jnp.exp2(-8.0 * jnp.arange(1, MOBA_HEADS + 1, dtype=F32) / MOBA_HEADS)

    q_aug, k_aug, v_pad, k_out, v_out = _moba_qkv_call(xp, g1, mp[0], mp[1], w_fused)
    hpc = 2
    slope_rows =jnp.repeat((slopes * LOG2E).reshape(MOBA_HEADS // hpc, hpc), MOBA_BLOCK, axis=1)
    al_tab = slope_rows[:, :, None] * jnp.arange(2 * MOBA_BLOCK, dtype=F32)
    sl_tab = jnp.broadcast_to(slope_rows[:, :, None], slope_rows.shape + (LANES,))
    o_pad = _moba_attn_call(q_aug, k_aug, v_pad, al_tab, sl_tab)
    xp = _mm_res(o_pad, wo_prompt, xp, mp[2], 512)
    xp = moe(xp, 0, mp, 1024, False)
    moba_k_prompt = k_out.reshape(1, seq, MOBA_KV_HEADS, MOBA_HEAD_DIM)
    moba_v_prompt = v_out.reshape(1, seq, MOBA_KV_HEADS, MOBA_HEAD_DIM)

    nq = MOBA_HEADS * MOBA_HEAD_DIM
    nk = MOBA_KV_HEADS * MOBA_HEAD_DIM
    qkv_s = _norm_matmul(xs, g1, msm[0], msm[1], moba_w_qkv.astype(BF16), 256)
    q_s = qkv_s[:, :nq].reshape(nb, tdec, MOBA_HEADS, 1, MOBA_HEAD_DIM) * MOBA_HEAD_DIM ** -0.5
    k_s = qkv_s[:, nq:nq + nk]
    v_s = qkv_s[:, nq + nk:]
    kv_of = jnp.arange(MOBA_HEADS) // MOBA_GROUP
    head_kv = (kv_of[:, None] == jnp.arange(MOBA_KV_HEADS)[None, :]).astype(F32)
    rows = tdec * MOBA_HEADS
    qbd = (q_s * head_kv[None, None, :, :, None]).reshape(nb, rows, nk).astype(BF16)
    slope_col = jnp.tile(slopes, tdec).reshape(rows, 1)
    tpos_col = jnp.repeat(past + jnp.arange(tdec, dtype=F32), MOBA_HEADS).reshape(rows, 1)
    hk_mask = jnp.tile(jnp.repeat(head_kv, MOBA_HEAD_DIM, axis=1), (tdec, 1))
    pad_new = lambda a: jnp.pad(a.reshape(nb, tdec, -1), ((0, 0), (0, LANES - tdec), (0, 0))).astype(BF16)
    om = _moba_sample_call(page_table, qbd, slope_col, tpos_col, hk_mask, pad_new(k_s), pad_new(v_s),
                           cache_moba_k.transpose(0, 2, 3, 1).reshape(n_pool, nk, page),
                           cache_moba_v.transpose(0, 2, 3, 1).reshape(n_pool, nk, page))
    xs = _mm_res(om.reshape(ms, MOBA_HEADS * nk), wo_sample, xs, msm[2], 256)
    xs = moe(xs, 0, msm, 512, False)
    moba_k_sample = k_s.reshape(nb, tdec, MOBA_KV_HEADS, MOBA_HEAD_DIM)
    moba_v_sample = v_s.reshape(nb, tdec, MOBA_KV_HEADS, MOBA_HEAD_DIM)

    mp, msm = mods(1)
    w1, w_lat, w_qr, w_qrr, w_vo = _mla_weights(mla_w_dq, mla_w_uq, mla_w_dkv, mla_w_uk, mla_w_uv, mla_w_o)
    g1 = row(g_norm1[1])
    gq = row(mla_g_q)
    gkv = row(mla_g_kv)

    cos_p, sin_p = _rope_tables(jnp.arange(seq))
    q_full, ckv_p, kr_p, k_full = _mla_qkv_call(xp, g1, mp[0], mp[1], cos_p, sin_p, w1, gq, gkv, w_lat, w_qr, w_qrr, 256)
    o_lat = _mla_attn_call(q_full, k_full, 256, 512, 2, 8)
    xp = _mm_res(o_lat, w_vo, xp, mp[2], 512)
    y_prompt = moe(xp, 1, mp, 1024, True).reshape(1, seq, d)

    cos_s, sin_s = _rope_tables(jnp.tile(past + jnp.arange(tdec), nb))
    q_fs, ckv_s, kr_s, k_fs = _mla_qkv_call(xs, g1, msm[0], msm[1], cos_s, sin_s, w1, gq, gkv, w_lat, w_qr, w_qrr, 256)
    q_rows = q_fs.reshape(MLA_HEADS, nb, tdec, 2 * LANES).transpose(1, 2, 0, 3).reshape(nb, tdec * MLA_HEADS, 2 * LANES)
    trow_col = jnp.repeat(jnp.arange(tdec, dtype=F32), MLA_HEADS).reshape(tdec * MLA_HEADS, 1)
    k_new = jnp.pad(k_fs.reshape(nb, tdec, 2 * LANES), ((0, 0), (0, LANES - tdec), (0, 0)))
    o_lat_s = _mla_sample_call(page_table, q_rows, trow_col, k_new, cache_mla_ckv, cache_mla_krope.transpose(0, 2, 1))
    xs = _mm_res(o_lat_s.reshape(ms, MLA_HEADS * LANES), w_vo, xs, msm[2], 256)
    y_sample = moe(xs, 1, msm, 512, True).reshape(nb, tdec, d)

    return (y_prompt, y_sample, moba_k_prompt, moba_v_prompt,
            ckv_p.reshape(1, seq, MLA_KV_LORA), kr_p.reshape(1, seq, MLA_ROPE),
            moba_k_sample, moba_v_sample,
            ckv_s.reshape(nb, tdec, MLA_KV_LORA), kr_s.reshape(nb, tdec, MLA_ROPE))
```

```python
import functools

import jax
import jax.numpy as jnp
from jax import lax
from jax.experimental import pallas as pl
from jax.experimental.pallas import tpu as pltpu

F32 = jnp.float32
BF16 = jnp.bfloat16
HIGHEST = lax.Precision.HIGHEST

D_MODEL = 1024
N_ADA = 6
NORM_EPS = 1e-6
MOBA_HEADS = 16
MOBA_KV_HEADS = 4
MOBA_GROUP = MOBA_HEADS // MOBA_KV_HEADS
MOBA_HEAD_DIM = 64
MOBA_BLOCK = 256
MOBA_TOPK = 3
MLA_HEADS = 16
MLA_Q_LORA = 256
MLA_KV_LORA = 128
MLA_NOPE = 64
MLA_ROPE = 32
MLA_V = 64
ROPE_THETA = 10000.0
N_GROUPS = 4
EXPERTS_PER_GROUP = 8
N_EXPERTS = N_GROUPS * EXPERTS_PER_GROUP
D_EXPERT = 256

LANES = 128
NEG = -1e30
LOG2E = 1.4426950408889634
VMEM_LIMIT = 56 * 1024 * 1024
MOBA_PAGES_PER_STEP = 32
PAGES_PER_STEP = 16

_NT = (((1,), (1,)), ((), ()))


def _nt_dot(a, b, **kw):
    return lax.dot_general(a, b, _NT, preferred_element_type=F32, **kw)


def _cparams(*sem):
    return pltpu.CompilerParams(dimension_semantics=sem, vmem_limit_bytes=VMEM_LIMIT)


def _rms_mod(x, g, shift, scale):
    y = x * lax.rsqrt(jnp.mean(x * x, axis=-1, keepdims=True) + NORM_EPS)
    return (y * g) * (1.0 + scale) + shift


def _top_mask_bias(vals, lanef, n_top, bias):
    for _ in range(n_top):
        m = jnp.max(vals, axis=-1, keepdims=True)
        idx = jnp.min(jnp.where(vals == m, lanef, 1e9), axis=-1, keepdims=True)
        idx = jnp.where(m > -jnp.inf, idx, -1.0)
        pick = lanef == idx
        bias = jnp.where(pick, 0.0, bias)
        vals = jnp.where(pick, -jnp.inf, vals)
    return bias


def _ada_kernel(c_ref, w_ref, b_ref, o_ref):
    c = c_ref[...]
    a = c * jax.nn.sigmoid(c)
    o_ref[0] = jnp.dot(a, w_ref[0], preferred_element_type=F32, precision=HIGHEST) + b_ref[0]


def _ada_call(c_all, w_ada, b_ada):
    depth, d, n = w_ada.shape
    rows = c_all.shape[0]
    tn = 512
    return pl.pallas_call(
        _ada_kernel,
        grid=(depth, n // tn),
        in_specs=[
            pl.BlockSpec((rows, d), lambda l, j: (0, 0)),
            pl.BlockSpec((1, d, tn), lambda l, j: (l, 0, j)),
            pl.BlockSpec((1, 1, tn), lambda l, j: (l, 0, j)),
        ],
        out_specs=pl.BlockSpec((1, rows, tn), lambda l, j: (l, 0, j)),
        out_shape=jax.ShapeDtypeStruct((depth, rows, n), F32),
        compiler_params=_cparams("arbitrary", "arbitrary"),
        name="ada_mod",
    )(c_all, w_ada, b_ada.reshape(depth, 1, n))


def _mod_spec(mod, tm):
    d = mod.shape[1]
    if mod.shape[0] == 1:
        return pl.BlockSpec((1, d), lambda i, *_: (0, 0))
    return pl.BlockSpec((tm, d), lambda i, *_: (i, 0))


def _norm_matmul_kernel(x_ref, g_ref, sh_ref, sc_ref, w_ref, o_ref):
    h = _rms_mod(x_ref[...], g_ref[...], sh_ref[...], sc_ref[...]).astype(BF16)
    o_ref[...] = jnp.dot(h, w_ref[...], preferred_element_type=F32)


def _norm_matmul(x, g, shift, scale, w, tm):
    m, d = x.shape
    tm = min(tm, m)
    n = w.shape[1]
    return pl.pallas_call(
        _norm_matmul_kernel,
        grid=(m // tm,),
        in_specs=[
            pl.BlockSpec((tm, d), lambda i: (i, 0)),
            pl.BlockSpec((1, d), lambda i: (0, 0)),
            _mod_spec(shift, tm),
            _mod_spec(scale, tm),
            pl.BlockSpec((d, n), lambda i: (0, 0)),
        ],
        out_specs=pl.BlockSpec((tm, n), lambda i: (i, 0)),
        out_shape=jax.ShapeDtypeStruct((m, n), F32),
        compiler_params=_cparams("arbitrary"),
        name="norm_matmul",
    )(x, g, shift, scale, w)


def _mm_res_kernel(a_ref, w_ref, x_ref, gate_ref, o_ref):
    y = jnp.dot(a_ref[...], w_ref[...], preferred_element_type=F32)
    o_ref[...] = x_ref[...] + gate_ref[...] * y


def _mm_res(a, w, x, gate, tm):
    m, k = a.shape
    tm = min(tm, m)
    n = w.shape[1]
    return pl.pallas_call(
        _mm_res_kernel,
        grid=(m // tm,),
        in_specs=[
            pl.BlockSpec((tm, k), lambda i: (i, 0)),
            pl.BlockSpec((k, n), lambda i: (0, 0)),
            pl.BlockSpec((tm, n), lambda i: (i, 0)),
            _mod_spec(gate, tm),
        ],
        out_specs=pl.BlockSpec((tm, n), lambda i: (i, 0)),
        out_shape=jax.ShapeDtypeStruct((m, n), F32),
        compiler_params=_cparams("arbitrary"),
        name="matmul_residual",
    )(a, w, x, gate)


def _bmm_kernel(a_ref, b_ref, o_ref):
    o_ref[0] = jnp.dot(a_ref[0], b_ref[0], preferred_element_type=F32, precision=HIGHEST)


def _bmm(a, b):
    n, m, k = a.shape
    p = b.shape[2]
    return pl.pallas_call(
        _bmm_kernel,
        grid=(n,),
        in_specs=[pl.BlockSpec((1, m, k), lambda i: (i, 0, 0)), pl.BlockSpec((1, k, p), lambda i: (i, 0, 0))],
        out_specs=pl.BlockSpec((1, m, p), lambda i: (i, 0, 0)),
        out_shape=jax.ShapeDtypeStruct((n, m, p), F32),
        compiler_params=_cparams("arbitrary"),
        name="fold_weights",
    )(a, b)


_QW = MOBA_HEADS * LANES
_KW = MOBA_KV_HEADS * LANES


def _moba_qkv_kernel(x_ref, g_ref, sh_ref, sc_ref, w_ref, q_ref, ka_ref, vp_ref, ko_ref, vo_ref, km_ref):
    i = pl.program_id(0)
    tm = x_ref.shape[0]
    half = LANES // 2

    @pl.when(i == 0)
    def _():
        km_ref[...] = jnp.zeros_like(km_ref)

    h = _rms_mod(x_ref[...], g_ref[...], sh_ref[...], sc_ref[...]).astype(BF16)
    y = jnp.dot(h, w_ref[...], preferred_element_type=F32)
    lane = lax.broadcasted_iota(jnp.int32, (tm, LANES), 1)
    lanef = lane.astype(F32)
    kvw = MOBA_KV_HEADS * MOBA_HEAD_DIM
    ko_ref[...] = y[:, _QW + 2 * _KW:_QW + 2 * _KW + kvw]
    vo_ref[...] = y[:, _QW + 2 * _KW + kvw:_QW + 2 * _KW + 2 * kvw]
    for k in range(MOBA_KV_HEADS):
        kp = y[:, _QW + k * LANES:_QW + (k + 1) * LANES]
        km_ref[k, pl.ds(half + i, 1), :] = jnp.mean(kp, axis=0, keepdims=True)
        ka_ref[:, k * LANES:(k + 1) * LANES] = jnp.where(lane == half + i, 1.0, kp).astype(BF16)
        vp = y[:, _QW + _KW + k * LANES:_QW + _KW + (k + 1) * LANES]
        vp_ref[:, k * LANES:(k + 1) * LANES] = jnp.where(lane == LANES - 1, 1.0, vp).astype(BF16)
    past = (lane >= half) & (lane < half + i)
    own_bias = jnp.where(lane == half + i, 0.0, NEG)
    for hd in range(MOBA_HEADS):
        qp = y[:, hd * LANES:(hd + 1) * LANES]
        gs = _nt_dot(qp, km_ref[hd // MOBA_GROUP], precision=HIGHEST)
        bias = _top_mask_bias(jnp.where(past, gs, -jnp.inf), lanef, MOBA_TOPK, own_bias)
        q_ref[hd] = jnp.where(lane < half, qp * (MOBA_HEAD_DIM ** -0.5 * LOG2E), bias).astype(BF16)


def _moba_qkv_call(x, g, shift, scale, w):
    t, d = x.shape
    tm = MOBA_BLOCK
    assert t // tm <= LANES // 2
    n = w.shape[1]
    kvw = MOBA_KV_HEADS * MOBA_HEAD_DIM
    return pl.pallas_call(
        _moba_qkv_kernel,
        grid=(t // tm,),
        in_specs=[
            pl.BlockSpec((tm, d), lambda i: (i, 0)),
            pl.BlockSpec((1, d), lambda i: (0, 0)),
            pl.BlockSpec((1, d), lambda i: (0, 0)),
            pl.BlockSpec((1, d), lambda i: (0, 0)),
            pl.BlockSpec((d, n), lambda i: (0, 0)),
        ],
        out_specs=[
            pl.BlockSpec((MOBA_HEADS, tm, LANES), lambda i: (0, i, 0)),
            pl.BlockSpec((tm, _KW), lambda i: (i, 0)),
            pl.BlockSpec((tm, _KW), lambda i: (i, 0)),
            pl.BlockSpec((tm, kvw), lambda i: (i, 0)),
            pl.BlockSpec((tm, kvw), lambda i: (i, 0)),
        ],
        out_shape=[
            jax.ShapeDtypeStruct((MOBA_HEADS, t, LANES), BF16),
            jax.ShapeDtypeStruct((t, _KW), BF16),
            jax.ShapeDtypeStruct((t, _KW), BF16),
            jax.ShapeDtypeStruct((t, kvw), F32),
            jax.ShapeDtypeStruct((t, kvw), F32),
        ],
        scratch_shapes=[pltpu.VMEM((MOBA_KV_HEADS, LANES, LANES), F32)],
        compiler_params=_cparams("arbitrary"),
        name="moba_qkv_gate",
    )(x, g, shift, scale, w)


def _moba_attn_kernel(q_ref, k_ref, v_ref, al_ref, sl_ref, o_ref, m_sc, acc_sc):
    i = pl.program_id(1)
    tq = q_ref.shape[1]
    blk = MOBA_BLOCK
    n_chains, rows, _ = m_sc.shape
    hpc = rows // tq
    m_sc[...] = jnp.full_like(m_sc, NEG)
    acc_sc[...] = jnp.zeros_like(acc_sc)

    def step(j, nblk, masked):
        tk = nblk * blk
        start = pl.multiple_of(j * blk, blk)
        dj = ((j - i) * blk).astype(F32)
        for u in range(n_chains):
            kv = u * hpc // MOBA_GROUP
            kb = k_ref[pl.ds(start, tk), kv * LANES:(kv + 1) * LANES]
            vb = v_ref[pl.ds(start, tk), kv * LANES:(kv + 1) * LANES]
            q = q_ref[u * hpc:(u + 1) * hpc].reshape(rows, LANES)
            s = _nt_dot(q, kb) + al_ref[u, :, :tk]
            if masked:
                row_pos = (nblk - 1) * blk + (lax.broadcasted_iota(jnp.int32, (rows, tk), 0) & (tq - 1))
                col = lax.broadcasted_iota(jnp.int32, (rows, tk), 1)
                s = jnp.where(col <= row_pos, s, NEG)
            c = sl_ref[u] * dj
            m_prev = m_sc[u]
            m_new = jnp.maximum(m_prev, jnp.max(s, axis=-1, keepdims=True) + c)
            shift = m_new - c
            p = jnp.exp2(s - jnp.concatenate([shift] * (tk // LANES), axis=1))
            acc_sc[u] = acc_sc[u] * jnp.exp2(m_prev - m_new) + jnp.dot(p.astype(BF16), vb, preferred_element_type=F32)
            m_sc[u] = m_new

    def past(jj, _):
        step(2 * jj, 2, False)
        return 0

    lax.fori_loop(0, i // 2, past, 0)

    @pl.when(i % 2 == 1)
    def _():
        step(i - 1, 2, True)

    @pl.when(i % 2 == 0)
    def _():
        step(i, 1, True)
    for u in range(n_chains):
        acc = acc_sc[u]
        o = acc / acc[:, LANES - 1:]
        for x in range(hpc):
            hd = u * hpc + x
            o_ref[:, hd * LANES:(hd + 1) * LANES] = o[x * tq:(x + 1) * tq].astype(BF16)


def _moba_attn_call(q_aug, k_aug, v_pad, al_tab, sl_tab):
    nh, t, _ = q_aug.shape
    tq = MOBA_BLOCK
    assert tq & (tq - 1) == 0
    kvps = 4
    n_steps = MOBA_KV_HEADS // kvps
    n_chains, rows = al_tab.shape[0] // n_steps, al_tab.shape[1]
    gw = kvps * MOBA_GROUP * LANES
    once = pl.Buffered(1)
    return pl.pallas_call(
        _moba_attn_kernel,
        grid=(n_steps, t // tq),
        in_specs=[
            pl.BlockSpec((kvps * MOBA_GROUP, tq, LANES), lambda k, i: (k, i, 0)),
            pl.BlockSpec((t, kvps * LANES), lambda k, i: (0, k), pipeline_mode=once),
            pl.BlockSpec((t, kvps * LANES), lambda k, i: (0, k), pipeline_mode=once),
            pl.BlockSpec((n_chains, rows, 2 * MOBA_BLOCK), lambda k, i: (k, 0, 0), pipeline_mode=once),
            pl.BlockSpec((n_chains, rows, LANES), lambda k, i: (k, 0, 0), pipeline_mode=once),
        ],
        out_specs=pl.BlockSpec((tq, gw), lambda k, i: (i, k)),
        out_shape=jax.ShapeDtypeStruct((t, nh * LANES), BF16),
        scratch_shapes=[pltpu.VMEM((n_chains, rows, LANES), F32), pltpu.VMEM((n_chains, rows, LANES), F32)],
        compiler_params=_cparams("arbitrary", "arbitrary"),
        name="moba_attn",
    )(q_aug, k_aug, v_pad, al_tab, sl_tab)


def _moe_kernel(x_ref, g_ref, sh_ref, sc_ref, gt_ref, gf_ref, wr_ref, wgu_ref, wdn_ref, o_ref,
                h_sc, gate_sc, acc_sc, *, final):
    e = pl.program_id(1)
    tm = x_ref.shape[0]
    lane = lax.broadcasted_iota(jnp.int32, (tm, LANES), 1)

    @pl.when(e == 0)
    def _():
        h = _rms_mod(x_ref[...], g_ref[...], sh_ref[...], sc_ref[...])
        h_sc[...] = h.astype(BF16)
        lanef = lane.astype(F32)
        logits = jnp.dot(h, wr_ref[...], preferred_element_type=F32, precision=HIGHEST)
        gl = jnp.where((lane >= N_EXPERTS) & (lane < N_EXPERTS + N_GROUPS), logits, -jnp.inf)
        gmax = jnp.max(gl, axis=-1, keepdims=True)
        g_sel = jnp.min(jnp.where(gl == gmax, lanef, 1e9), axis=-1, keepdims=True) - N_EXPERTS
        g_p = 1.0 / jnp.sum(jnp.exp(gl - gmax), axis=-1, keepdims=True)
        el = logits
        lo = g_sel * EXPERTS_PER_GROUP
        el = jnp.where((lanef >= lo) & (lanef < lo + EXPERTS_PER_GROUP), el, -jnp.inf)
        emax = jnp.max(el, axis=-1, keepdims=True)
        ex = jnp.exp(el - emax)
        prob = ex / jnp.sum(ex, axis=-1, keepdims=True)
        prob = jnp.where(el > -jnp.inf, prob, -jnp.inf)
        p1 = jnp.max(prob, axis=-1, keepdims=True)
        i1 = jnp.min(jnp.where(prob == p1, lanef, 1e9), axis=-1, keepdims=True)
        prob2 = jnp.where(lanef == i1, -jnp.inf, prob)
        p2 = jnp.max(prob2, axis=-1, keepdims=True)
        i2 = jnp.min(jnp.where(prob2 == p2, lanef, 1e9), axis=-1, keepdims=True)
        den = p1 + p2
        gate_sc[...] = jnp.where(lanef == i1, p1 / den * g_p, jnp.where(lanef == i2, p2 / den * g_p, 0.0))
        acc_sc[...] = jnp.zeros_like(acc_sc)

    gcol = jnp.sum(jnp.where(lane == e, gate_sc[...], 0.0), axis=-1, keepdims=True)
    gu = jnp.dot(h_sc[...], wgu_ref[0], preferred_element_type=F32)
    gg = gu[:, :D_EXPERT]
    a = (gg * jax.nn.sigmoid(gg)) * gu[:, D_EXPERT:] * gcol
    acc_sc[...] += jnp.dot(a.astype(BF16), wdn_ref[0], preferred_element_type=F32)

    @pl.when(e == pl.num_programs(1) - 1)
    def _():
        y = x_ref[...] + gt_ref[...] * acc_sc[...]
        if final:
            y = y * lax.rsqrt(jnp.mean(y * y, axis=-1, keepdims=True) + NORM_EPS) * gf_ref[...]
        o_ref[...] = y


def _moe_call(x, g, shift, scale, gate, g_final, w_route, w_gu, w_down, tm, final):
    m, d = x.shape
    tm = min(tm, m)
    ne, _, f2 = w_gu.shape
    fd = w_down.shape[1]
    return pl.pallas_call(
        functools.partial(_moe_kernel, final=final),
        grid=(m // tm, ne),
        in_specs=[
            pl.BlockSpec((tm, d), lambda i, e: (i, 0)),
            pl.BlockSpec((1, d), lambda i, e: (0, 0)),
            _mod_spec(shift, tm),
            _mod_spec(scale, tm),
            _mod_spec(gate, tm),
            pl.BlockSpec((1, d), lambda i, e: (0, 0)),
            pl.BlockSpec((d, LANES), lambda i, e: (0, 0)),
            pl.BlockSpec((1, d, f2), lambda i, e: (e, 0, 0)),
            pl.BlockSpec((1, fd, d), lambda i, e: (e, 0, 0)),
        ],
        out_specs=pl.BlockSpec((tm, d), lambda i, e: (i, 0)),
        out_shape=jax.ShapeDtypeStruct((m, d), F32),
        scratch_shapes=[pltpu.VMEM((tm, d), BF16), pltpu.VMEM((tm, LANES), F32), pltpu.VMEM((tm, d), F32)],
        compiler_params=_cparams("arbitrary", "arbitrary"),
        name="hier_moe",
    )(x, g, shift, scale, gate, g_final, w_route, w_gu, w_down)


_MLA_W1 = MLA_Q_LORA + MLA_KV_LORA + 2 * LANES


def _mla_qkv_kernel(x_ref, g_ref, sh_ref, sc_ref, cos_ref, sin_ref, w1_ref, gq_ref, gkv_ref, wl_ref, wr_ref, wrr_ref,
                    q_ref, ckv_ref, kr_ref, kf_ref):
    h = _rms_mod(x_ref[...], g_ref[...], sh_ref[...], sc_ref[...]).astype(BF16)
    a = jnp.dot(h, w1_ref[...], preferred_element_type=F32)
    cos = cos_ref[...]
    sin = sin_ref[...]
    cq = a[:, :MLA_Q_LORA]
    cqn = (cq * lax.rsqrt(jnp.mean(cq * cq, axis=-1, keepdims=True) + NORM_EPS) * gq_ref[...]).astype(BF16)
    kvc = a[:, MLA_Q_LORA:MLA_Q_LORA + MLA_KV_LORA]
    ckv = kvc * lax.rsqrt(jnp.mean(kvc * kvc, axis=-1, keepdims=True) + NORM_EPS) * gkv_ref[...]
    o = MLA_Q_LORA + MLA_KV_LORA
    kr = a[:, o:o + LANES] * cos + a[:, o + LANES:o + 2 * LANES] * sin
    ckv_ref[...] = ckv
    kr_ref[...] = kr[:, :MLA_ROPE]
    kf_ref[:, :LANES] = ckv.astype(BF16)
    lane = lax.broadcasted_iota(jnp.int32, kr.shape, 1)
    kf_ref[:, LANES:] = jnp.where(lane == LANES - 1, 1.0, kr).astype(BF16)
    scale = (MLA_NOPE + MLA_ROPE) ** -0.5 * LOG2E
    ql = jnp.dot(cqn, wl_ref[...], preferred_element_type=F32)
    qr = jnp.dot(cqn, wr_ref[...], preferred_element_type=F32)
    qrr = jnp.dot(cqn, wrr_ref[...], preferred_element_type=F32)
    for hd in range(MLA_HEADS):
        sl = slice(hd * LANES, (hd + 1) * LANES)
        q_ref[hd, :, :LANES] = (ql[:, sl] * scale).astype(BF16)
        q_ref[hd, :, LANES:] = ((qr[:, sl] * cos + qrr[:, sl] * sin) * scale).astype(BF16)


def _mla_qkv_call(x, g, shift, scale, cos, sin, w1, gq, gkv, wl, wr, wrr, tm):
    m, d = x.shape
    tm = min(tm, m)
    hw = MLA_HEADS * LANES
    full = lambda shape: pl.BlockSpec(shape, lambda i: (0,) * len(shape))
    return pl.pallas_call(
        _mla_qkv_kernel,
        grid=(m // tm,),
        in_specs=[
            pl.BlockSpec((tm, d), lambda i: (i, 0)),
            full((1, d)),
            _mod_spec(shift, tm),
            _mod_spec(scale, tm),
            pl.BlockSpec((tm, LANES), lambda i: (i, 0)),
            pl.BlockSpec((tm, LANES), lambda i: (i, 0)),
            full((d, _MLA_W1)),
            full((1, MLA_Q_LORA)),
            full((1, MLA_KV_LORA)),
            full((MLA_Q_LORA, hw)),
            full((MLA_Q_LORA, hw)),
            full((MLA_Q_LORA, hw)),
        ],
        out_specs=[
            pl.BlockSpec((MLA_HEADS, tm, 2 * LANES), lambda i: (0, i, 0)),
            pl.BlockSpec((tm, MLA_KV_LORA), lambda i: (i, 0)),
            pl.BlockSpec((tm, MLA_ROPE), lambda i: (i, 0)),
            pl.BlockSpec((tm, 2 * LANES), lambda i: (i, 0)),
        ],
        out_shape=[
            jax.ShapeDtypeStruct((MLA_HEADS, m, 2 * LANES), BF16),
            jax.ShapeDtypeStruct((m, MLA_KV_LORA), F32),
            jax.ShapeDtypeStruct((m, MLA_ROPE), F32),
            jax.ShapeDtypeStruct((m, 2 * LANES), BF16),
        ],
        compiler_params=_cparams("arbitrary"),
        name="mla_qkv",
    )(x, g, shift, scale, cos, sin, w1, gq, gkv, wl, wr, wrr)


def _mla_attn_kernel(q_ref, k_ref, o_ref, m_sc, acc_sc, *, tk, hpc, cpb):
    i = pl.program_id(0)
    nh, tq, w = q_ref.shape
    rows = hpc * tq
    n_chains = nh // hpc
    n_past = (i * tq) // tk
    m_sc[...] = jnp.full_like(m_sc, NEG)
    acc_sc[...] = jnp.zeros_like(acc_sc)

    def tile(j, masked):
        start = pl.multiple_of(j * tk, tk)
        kb = k_ref[pl.ds(start, tk), :]

        def chain_pair(cp, _):
            for u in range(cpb):
                c = cp * cpb + u
                q = q_ref[pl.ds(c * hpc, hpc)].reshape(rows, w)
                s = _nt_dot(q, kb)
                if masked:
                    row_pos = i * tq + (lax.broadcasted_iota(jnp.int32, (rows, tk), 0) & (tq - 1))
                    col_pos = start + lax.broadcasted_iota(jnp.int32, (rows, tk), 1)
                    s = jnp.where(col_pos <= row_pos, s, NEG)
                m_prev = m_sc[c]
                m_new = jnp.maximum(m_prev, jnp.max(s, axis=-1, keepdims=True))
                p = jnp.exp2(s - jnp.concatenate([m_new] * (tk // LANES), axis=1))
                alpha = jnp.exp2(m_prev - m_new)
                acc_sc[c] = (acc_sc[c] * jnp.concatenate([alpha] * (w // LANES), axis=1)
                             + jnp.dot(p.astype(BF16), kb, preferred_element_type=F32))
                m_sc[c] = m_new
            return 0

        lax.fori_loop(0, n_chains // cpb, chain_pair, 0)

    def past(j, _):
        tile(j, False)
        return 0

    lax.fori_loop(0, n_past, past, 0)
    tile(n_past, True)
    for c in range(n_chains):
        acc = acc_sc[c]
        o = acc[:, :LANES] / acc[:, w - 1:]
        for x in range(hpc):
            hd = c * hpc + x
            o_ref[:, hd * LANES:(hd + 1) * LANES] = o[x * tq:(x + 1) * tq].astype(BF16)


def _mla_attn_call(q_full, k_full, tq, tk, hpc, cpb):
    nh, t, w = q_full.shape
    assert tq & (tq - 1) == 0 and tk % tq == 0 and t % tk == 0 and nh % (cpb * hpc) == 0
    n_chains, rows = nh // hpc, hpc * tq
    return pl.pallas_call(
        functools.partial(_mla_attn_kernel, tk=tk, hpc=hpc, cpb=cpb),
        grid=(t // tq,),
        in_specs=[
            pl.BlockSpec((nh, tq, w), lambda i: (0, i, 0)),
            pl.BlockSpec((t, w), lambda i: (0, 0)),
        ],
        out_specs=pl.BlockSpec((tq, nh * LANES), lambda i: (i, 0)),
        out_shape=jax.ShapeDtypeStruct((t, nh * LANES), BF16),
        scratch_shapes=[pltpu.VMEM((n_chains, rows, LANES), F32), pltpu.VMEM((n_chains, rows, w), F32)],
        compiler_params=_cparams("arbitrary"),
        name="mla_attn",
    )(q_full, k_full)


def _moba_sample_kernel(pt_ref, q_ref, slope_ref, tpos_ref, hk_ref, kn_ref, vn_ref, k_hbm, v_hbm, o_ref,
                        kbuf, vbuf, ksem, vsem, s_sc, p_sc, g_sc, b_sc, pn_sc, l_sc, *, n_pages, past_len):
    pps = MOBA_PAGES_PER_STEP
    seq = pl.program_id(0)
    n_seq = pl.num_programs(0)
    n_chunks = n_pages // pps
    n_blocks = n_pages // 2
    q = q_ref[0]
    rows = q.shape[0]
    lane = lax.broadcasted_iota(jnp.int32, (rows, LANES), 1)
    lanef = lane.astype(F32)

    def k_copies(sq, c):
        return [pltpu.make_async_copy(k_hbm.at[pt_ref[sq, c * pps + p]], kbuf.at[c, p], ksem.at[c])
                for p in range(pps)]

    def v_copies(sq, c):
        return [pltpu.make_async_copy(v_hbm.at[pt_ref[sq, c * pps + p]], vbuf.at[c, p], vsem.at[c])
                for p in range(pps)]

    def start(copies):
        for cp in copies:
            cp.start()

    def wait(copies):
        for cp in copies:
            cp.wait()

    @pl.when(seq == 0)
    def _():
        for c in range(n_chunks):
            start(k_copies(0, c))

    for c in range(n_chunks):
        start(v_copies(seq, c))

    for c in range(n_chunks):
        wait(k_copies(seq, c))
        for a in range(pps // 2):
            kb = jnp.concatenate([kbuf[c, 2 * a].astype(BF16), kbuf[c, 2 * a + 1].astype(BF16)], axis=1)
            s2 = jnp.dot(q, kb, preferred_element_type=F32)
            s_sc[c * pps + 2 * a] = s2[:, :LANES]
            s_sc[c * pps + 2 * a + 1] = s2[:, LANES:]

    @pl.when(seq + 1 < n_seq)
    def _():
        for c in range(n_chunks):
            start(k_copies(seq + 1, c))

    def softmax():
        for b in range(n_blocks):
            r = jnp.sum(s_sc[2 * b] + s_sc[2 * b + 1], axis=-1, keepdims=True)
            g_sc[b] = jnp.broadcast_to(r, (rows, LANES))
            b_sc[b] = jnp.full((rows, LANES), NEG, F32)
        for _ in range(min(MOBA_TOPK, n_blocks)):
            best = g_sc[0]
            for b in range(1, n_blocks):
                best = jnp.maximum(best, g_sc[b])
            idx = jnp.full((rows, LANES), float(n_blocks), F32)
            for b in reversed(range(n_blocks)):
                idx = jnp.where(g_sc[b] == best, float(b), idx)
            for b in range(n_blocks):
                hit = idx == float(b)
                b_sc[b] = jnp.where(hit, 0.0, b_sc[b])
                g_sc[b] = jnp.where(hit, -jnp.inf, g_sc[b])
        slope = jnp.broadcast_to(slope_ref[...], (rows, LANES))
        tpos = jnp.broadcast_to(tpos_ref[...], (rows, LANES))
        al_lane = slope * lanef
        sn = _nt_dot(q, kn_ref[0]) - slope * (tpos - (past_len + lanef))
        sn = jnp.where(past_len + lanef <= tpos, sn, NEG)
        mv = sn
        for b in range(n_blocks):
            for u in range(2):
                pg = 2 * b + u
                sc = s_sc[pg] + al_lane + (b_sc[b] + slope * (float(pg * LANES) - tpos))
                s_sc[pg] = sc
                mv = jnp.maximum(mv, sc)
        m = jnp.broadcast_to(jnp.max(mv, axis=-1, keepdims=True), (rows, LANES))
        pn = jnp.exp(sn - m)
        lv = pn
        for pg in range(n_pages):
            p = jnp.exp(s_sc[pg] - m)
            p_sc[pg] = p.astype(BF16)
            lv = lv + p
        l_sc[...] = jnp.sum(lv, axis=-1, keepdims=True)
        pn_sc[...] = pn.astype(BF16)

    softmax()
    acc = jnp.dot(pn_sc[...], vn_ref[0], preferred_element_type=F32)
    for c in range(n_chunks):
        wait(v_copies(seq, c))
        for a in range(pps // 2):
            pg = c * pps + 2 * a
            p2 = jnp.concatenate([p_sc[pg], p_sc[pg + 1]], axis=1)
            vt = jnp.concatenate([vbuf[c, 2 * a].astype(BF16), vbuf[c, 2 * a + 1].astype(BF16)], axis=1)
            acc += _nt_dot(p2, vt)
    o_ref[0] = (acc / l_sc[...] * hk_ref[...]).astype(BF16)


def _moba_sample_call(page_table, qbd, slope_col, tpos_col, hk_mask, k_new, v_new, cache_k, cache_v):
    nb, n_pages = page_table.shape
    rows, w = qbd.shape[1:]
    page = cache_k.shape[2]
    pps = MOBA_PAGES_PER_STEP
    assert page == LANES and n_pages % pps == 0 and pps % 2 == 0

    per_seq = lambda shape: pl.BlockSpec((1,) + shape, lambda b, pt: (b, 0, 0))
    const = lambda shape: pl.BlockSpec(shape, lambda b, pt: (0, 0))
    hbm = pl.BlockSpec(memory_space=pl.ANY)
    grid_spec = pltpu.PrefetchScalarGridSpec(
        num_scalar_prefetch=1,
        grid=(nb,),
        in_specs=[per_seq((rows, w)), const((rows, 1)), const((rows, 1)), const((rows, w)),
                  per_seq((LANES, w)), per_seq((LANES, w)), hbm, hbm],
        out_specs=per_seq((rows, w)),
        scratch_shapes=[
            pltpu.VMEM((n_pages // pps, pps, w, page), F32),
            pltpu.VMEM((n_pages // pps, pps, w, page), F32),
            pltpu.SemaphoreType.DMA((n_pages // pps,)),
            pltpu.SemaphoreType.DMA((n_pages // pps,)),
            pltpu.VMEM((n_pages, rows, LANES), F32),
            pltpu.VMEM((n_pages, rows, LANES), BF16),
            pltpu.VMEM((n_pages // 2, rows, LANES), F32),
            pltpu.VMEM((n_pages // 2, rows, LANES), F32),
            pltpu.VMEM((rows, LANES), BF16),
            pltpu.VMEM((rows, 1), F32),
        ],
    )
    return pl.pallas_call(
        functools.partial(_moba_sample_kernel, n_pages=n_pages, past_len=n_pages * page),
        grid_spec=grid_spec,
        out_shape=jax.ShapeDtypeStruct((nb, rows, w), BF16),
        compiler_params=_cparams("arbitrary"),
        name="moba_sample_attn",
    )(page_table, qbd, slope_col, tpos_col, hk_mask, k_new, v_new, cache_k, cache_v)


def _mla_sample_kernel(pt_ref, q_ref, trow_ref, kn_ref, c_hbm, r_hbm, o_ref, cbuf, rbuf, sem, cb_sc, s_sc, *, n_pages):
    cpp = PAGES_PER_STEP
    n_chunks = n_pages // cpp
    kw = cpp * LANES
    seq = pl.program_id(0)
    n_seq = pl.num_programs(0)
    slot = seq % 2
    q = q_ref[0]
    ql = q[:, :LANES]
    qr = q[:, LANES:LANES + MLA_ROPE]
    rows = q.shape[0]

    def seq_copies(sq, sl):
        out = []
        for p in range(n_pages):
            pid = pt_ref[sq, p]
            out.append(pltpu.make_async_copy(c_hbm.at[pid], cbuf.at[sl, p], sem.at[0, sl]))
            out.append(pltpu.make_async_copy(r_hbm.at[pid], rbuf.at[sl, p], sem.at[1, sl]))
        return out

    @pl.when(seq == 0)
    def _():
        for cp in seq_copies(0, 0):
            cp.start()

    for cp in seq_copies(seq, slot):
        cp.wait()

    @pl.when(seq + 1 < n_seq)
    def _():
        for cp in seq_copies(seq + 1, 1 - slot):
            cp.start()

    mv = jnp.full((rows, kw), NEG, F32)
    for c in range(n_chunks):
        cb = cbuf[slot, c * cpp:(c + 1) * cpp].reshape(kw, MLA_KV_LORA).astype(BF16)
        cb_sc[c] = cb
        rb = jnp.concatenate([rbuf[slot, c * cpp + p].astype(BF16) for p in range(cpp)], axis=1)
        s = _nt_dot(ql, cb) + jnp.dot(qr, rb, preferred_element_type=F32)
        s_sc[c] = s
        mv = jnp.maximum(mv, s)
    kn = kn_ref[0]
    lanef = lax.broadcasted_iota(jnp.int32, (rows, LANES), 1).astype(F32)
    sn = jnp.where(lanef <= trow_ref[...], _nt_dot(q, kn), NEG)
    m = jnp.maximum(jnp.max(mv, axis=-1, keepdims=True), jnp.max(sn, axis=-1, keepdims=True))
    pn = jnp.exp2(sn - m)
    acc = jnp.dot(pn.astype(BF16), kn[:, :LANES], preferred_element_type=F32)
    lv = jnp.zeros((rows, kw), F32)
    for c in range(n_chunks):
        pr = jnp.exp2(s_sc[c] - m)
        lv = lv + pr
        acc = acc + jnp.dot(pr.astype(BF16), cb_sc[c], preferred_element_type=F32)
    l = jnp.sum(lv, axis=-1, keepdims=True) + jnp.sum(pn, axis=-1, keepdims=True)
    o_ref[0] = (acc / l).astype(BF16)


def _mla_sample_call(page_table, q_rows, trow_col, k_new, cache_ckv, cache_krope):
    nb, n_pages = page_table.shape
    rows, w = q_rows.shape[1:]
    page = cache_ckv.shape[1]
    cpp = PAGES_PER_STEP
    assert page == LANES and n_pages % cpp == 0
    n_chunks = n_pages // cpp

    per_seq = lambda shape: pl.BlockSpec((1,) + shape, lambda b, pt: (b, 0, 0))
    hbm = pl.BlockSpec(memory_space=pl.ANY)
    grid_spec = pltpu.PrefetchScalarGridSpec(
        num_scalar_prefetch=1,
        grid=(nb,),
        in_specs=[per_seq((rows, w)), pl.BlockSpec((rows, 1), lambda b, pt: (0, 0)), per_seq((LANES, w)), hbm, hbm],
        out_specs=per_seq((rows, LANES)),
        scratch_shapes=[
            pltpu.VMEM((2, n_pages, page, MLA_KV_LORA), F32),
            pltpu.VMEM((2, n_pages, MLA_ROPE, page), F32),
            pltpu.SemaphoreType.DMA((2, 2)),
            pltpu.VMEM((n_chunks, cpp * page, MLA_KV_LORA), BF16),
            pltpu.VMEM((n_chunks, rows, cpp * page), F32),
        ],
    )
    return pl.pallas_call(
        functools.partial(_mla_sample_kernel, n_pages=n_pages),
        grid_spec=grid_spec,
        out_shape=jax.ShapeDtypeStruct((nb, rows, LANES), BF16),
        compiler_params=_cparams("arbitrary"),
        name="mla_sample_attn",
    )(page_table, q_rows, trow_col, k_new, cache_ckv, cache_krope)


def _pad_last(w, width):
    return jnp.pad(w, [(0, 0)] * (w.ndim - 1) + [(0, width - w.shape[-1])])


def _moba_weights(w_qkv, w_o):
    d = w_qkv.shape[0]
    nq = MOBA_HEADS * MOBA_HEAD_DIM
    nk = MOBA_KV_HEADS * MOBA_HEAD_DIM
    wq = _pad_last(w_qkv[:, :nq].reshape(d, MOBA_HEADS, MOBA_HEAD_DIM), LANES).reshape(d, _QW)
    wk = w_qkv[:, nq:nq + nk]
    wv = w_qkv[:, nq + nk:]
    wkp = _pad_last(wk.reshape(d, MOBA_KV_HEADS, MOBA_HEAD_DIM), LANES).reshape(d, _KW)
    wvp = _pad_last(wv.reshape(d, MOBA_KV_HEADS, MOBA_HEAD_DIM), LANES).reshape(d, _KW)
    w_fused = jnp.concatenate([wq, wkp, wvp, wk, wv], axis=1).astype(BF16)
    wo = w_o.reshape(MOBA_HEADS, MOBA_HEAD_DIM, d)
    wo_prompt = jnp.pad(wo, ((0, 0), (0, LANES - MOBA_HEAD_DIM), (0, 0))).reshape(_QW, d).astype(BF16)
    kv_of = jnp.arange(MOBA_HEADS) // MOBA_GROUP
    onehot = (kv_of[:, None] == jnp.arange(MOBA_KV_HEADS)[None, :]).astype(F32)
    wo_sample = (onehot[:, :, None, None] * wo[:, None, :, :]).reshape(MOBA_HEADS * nk, d).astype(BF16)
    return w_fused, wo_prompt, wo_sample


def _rotate_half_cols(w):
    half = w.shape[-1] // 2
    return jnp.concatenate([-w[..., half:], w[..., :half]], axis=-1)


def _mla_weights(w_dq, w_uq, w_dkv, w_uk, w_uv, w_o):
    d = w_dq.shape[0]
    w_r = w_dkv[:, MLA_KV_LORA:]
    w1 = jnp.concatenate([w_dq, w_dkv[:, :MLA_KV_LORA], _pad_last(w_r, LANES), _pad_last(_rotate_half_cols(w_r), LANES)],
                         axis=1).astype(BF16)
    wq = w_uq.reshape(MLA_Q_LORA, MLA_HEADS, MLA_NOPE + MLA_ROPE)
    wq_nope = wq[:, :, :MLA_NOPE].transpose(1, 0, 2)
    wuk_t = w_uk.transpose(1, 2, 0)
    w_lat = _bmm(wq_nope, wuk_t).transpose(1, 0, 2).reshape(MLA_Q_LORA, MLA_HEADS * LANES).astype(BF16)
    wq_rope = wq[:, :, MLA_NOPE:]
    w_qr = _pad_last(wq_rope, LANES).reshape(MLA_Q_LORA, MLA_HEADS * LANES).astype(BF16)
    w_qrr = _pad_last(_rotate_half_cols(wq_rope), LANES).reshape(MLA_Q_LORA, MLA_HEADS * LANES).astype(BF16)
    wuv = w_uv.transpose(1, 0, 2)
    wo = w_o.reshape(MLA_HEADS, MLA_V, d)
    w_vo = _bmm(wuv, wo).reshape(MLA_HEADS * MLA_KV_LORA, d).astype(BF16)
    return w1, w_lat, w_qr, w_qrr, w_vo


def _rope_tables(pos):
    inv = ROPE_THETA ** (-jnp.arange(0, MLA_ROPE, 2, dtype=F32) / MLA_ROPE)
    ang = pos.astype(F32)[:, None] * inv[None, :]
    cos = jnp.cos(ang)
    sin = jnp.sin(ang)
    return _pad_last(jnp.concatenate([cos, cos], axis=-1), LANES), _pad_last(jnp.concatenate([sin, sin], axis=-1), LANES)


def kernel(x_prompt, x_sample, cache_moba_k, cache_moba_v, cache_mla_ckv, cache_mla_krope, page_table, c_prompt, c_sample, w_ada, b_ada, g_norm1, g_norm2, moba_w_qkv, moba_w_o, mla_w_dq, mla_g_q, mla_w_uq, mla_w_dkv, mla_g_kv, mla_w_uk, mla_w_uv, mla_w_o, moe_w_group, moe_w_expert, moe_w_gu, moe_w_down, g_final):
    bp, seq, d = x_prompt.shape
    assert bp == 1
    nb, tdec, _ = x_sample.shape
    n_pool, page = cache_moba_k.shape[:2]
    n_pages = page_table.shape[1]
    past = n_pages * page
    ms = nb * tdec
    xp = x_prompt.reshape(seq, d)
    xs = x_sample.reshape(ms, d)

    c_all = jnp.concatenate([c_sample, c_prompt, jnp.zeros((7, d), F32)], axis=0)
    ada = _ada_call(c_all, w_ada, b_ada).reshape(w_ada.shape[0], c_all.shape[0], N_ADA, d)

    def mods(layer):
        mp = [ada[layer, nb:nb + 1, i] for i in range(N_ADA)]
        msm = [jnp.repeat(ada[layer, :nb, i], tdec, axis=0) for i in range(N_ADA)]
        return mp, msm

    row = lambda v: v.reshape(1, -1)
    w_route = [_pad_last(jnp.concatenate([moe_w_expert[l], moe_w_group[l]], axis=1), LANES) for l in range(2)]
    wgu = moe_w_gu.astype(BF16)
    wdn = moe_w_down.astype(BF16)
    gfin = row(g_final)

    def moe(x, layer, m, tm, final):
        return _moe_call(x, row(g_norm2[layer]), m[3], m[4], m[5], gfin, w_route[layer], wgu[layer], wdn[layer],
                         tm, final)

    mp, msm = mods(0)
    w_fused, wo_prompt, wo_sample = _moba_weights(moba_w_qkv, moba_w_o)
    g1 = row(g_norm1[0])
    slopes = jnp.exp2(-8.0 * jnp.arange(1, MOBA_HEADS + 1, dtype=F32) / MOBA_HEADS)

    q_aug, k_aug, v_pad, k_out, v_out = _moba_qkv_call(xp, g1, mp[0], mp[1], w_fused)
    hpc = 2
    slope_rows =jnp.repeat((slopes * LOG2E).reshape(MOBA_HEADS // hpc, hpc), MOBA_BLOCK, axis=1)
    al_tab = slope_rows[:, :, None] * jnp.arange(2 * MOBA_BLOCK, dtype=F32)
    sl_tab = jnp.broadcast_to(slope_rows[:, :, None], slope_rows.shape + (LANES,))
    o_pad = _moba_attn_call(q_aug, k_aug, v_pad, al_tab, sl_tab)
    xp = _mm_res(o_pad, wo_prompt, xp, mp[2], 512)
    xp = moe(xp, 0, mp, 1024, False)
    moba_k_prompt = k_out.reshape(1, seq, MOBA_KV_HEADS, MOBA_HEAD_DIM)
    moba_v_prompt = v_out.reshape(1, seq, MOBA_KV_HEADS, MOBA_HEAD_DIM)

    nq = MOBA_HEADS * MOBA_HEAD_DIM
    nk = MOBA_KV_HEADS * MOBA_HEAD_DIM
    qkv_s = _norm_matmul(xs, g1, msm[0], msm[1], moba_w_qkv.astype(BF16), 256)
    q_s = qkv_s[:, :nq].reshape(nb, tdec, MOBA_HEADS, 1, MOBA_HEAD_DIM) * MOBA_HEAD_DIM ** -0.5
    k_s = qkv_s[:, nq:nq + nk]
    v_s = qkv_s[:, nq + nk:]
    kv_of = jnp.arange(MOBA_HEADS) // MOBA_GROUP
    head_kv = (kv_of[:, None] == jnp.arange(MOBA_KV_HEADS)[None, :]).astype(F32)
    rows = tdec * MOBA_HEADS
    qbd = (q_s * head_kv[None, None, :, :, None]).reshape(nb, rows, nk).astype(BF16)
    slope_col = jnp.tile(slopes, tdec).reshape(rows, 1)
    tpos_col = jnp.repeat(past + jnp.arange(tdec, dtype=F32), MOBA_HEADS).reshape(rows, 1)
    hk_mask = jnp.tile(jnp.repeat(head_kv, MOBA_HEAD_DIM, axis=1), (tdec, 1))
    pad_new = lambda a: jnp.pad(a.reshape(nb, tdec, -1), ((0, 0), (0, LANES - tdec), (0, 0))).astype(BF16)
    om = _moba_sample_call(page_table, qbd, slope_col, tpos_col, hk_mask, pad_new(k_s), pad_new(v_s),
                           cache_moba_k.transpose(0, 2, 3, 1).reshape(n_pool, nk, page),
                           cache_moba_v.transpose(0, 2, 3, 1).reshape(n_pool, nk, page))
    xs = _mm_res(om.reshape(ms, MOBA_HEADS * nk), wo_sample, xs, msm[2], 256)
    xs = moe(xs, 0, msm, 512, False)
    moba_k_sample = k_s.reshape(nb, tdec, MOBA_KV_HEADS, MOBA_HEAD_DIM)
    moba_v_sample = v_s.reshape(nb, tdec, MOBA_KV_HEADS, MOBA_HEAD_DIM)

    mp, msm = mods(1)
    w1, w_lat, w_qr, w_qrr, w_vo = _mla_weights(mla_w_dq, mla_w_uq, mla_w_dkv, mla_w_uk, mla_w_uv, mla_w_o)
    g1 = row(g_norm1[1])
    gq = row(mla_g_q)
    gkv = row(mla_g_kv)

    cos_p, sin_p = _rope_tables(jnp.arange(seq))
    q_full, ckv_p, kr_p, k_full = _mla_qkv_call(xp, g1, mp[0], mp[1], cos_p, sin_p, w1, gq, gkv, w_lat, w_qr, w_qrr, 256)
    o_lat = _mla_attn_call(q_full, k_full, 256, 512, 2, 8)
    xp = _mm_res(o_lat, w_vo, xp, mp[2], 512)
    y_prompt = moe(xp, 1, mp, 1024, True).reshape(1, seq, d)

    cos_s, sin_s = _rope_tables(jnp.tile(past + jnp.arange(tdec), nb))
    q_fs, ckv_s, kr_s, k_fs = _mla_qkv_call(xs, g1, msm[0], msm[1], cos_s, sin_s, w1, gq, gkv, w_lat, w_qr, w_qrr, 256)
    q_rows = q_fs.reshape(MLA_HEADS, nb, tdec, 2 * LANES).transpose(1, 2, 0, 3).reshape(nb, tdec * MLA_HEADS, 2 * LANES)
    trow_col = jnp.repeat(jnp.arange(tdec, dtype=F32), MLA_HEADS).reshape(tdec * MLA_HEADS, 1)
    k_new = jnp.pad(k_fs.reshape(nb, tdec, 2 * LANES), ((0, 0), (0, LANES - tdec), (0, 0)))
    o_lat_s = _mla_sample_call(page_table, q_rows, trow_col, k_new, cache_mla_ckv, cache_mla_krope.transpose(0, 2, 1))
    xs = _mm_res(o_lat_s.reshape(ms, MLA_HEADS * LANES), w_vo, xs, msm[2], 256)
    y_sample = moe(xs, 1, msm, 512, True).reshape(nb, tdec, d)

    return (y_prompt, y_sample, moba_k_prompt, moba_v_prompt,
            ckv_p.reshape(1, seq, MLA_KV_LORA), kr_p.reshape(1, seq, MLA_ROPE),
            moba_k_sample, moba_v_sample,
            ckv_s.reshape(nb, tdec, MLA_KV_LORA), kr_s.reshape(nb, tdec, MLA_ROPE))
```

```python
import functools

import jax
import jax.numpy as jnp
from jax import lax
from jax.experimental import pallas as pl
from jax.experimental.pallas import tpu as pltpu

F32 = jnp.float32
BF16 = jnp.bfloat16
HIGHEST = lax.Precision.HIGHEST

D_MODEL = 1024
N_ADA = 6
NORM_EPS = 1e-6
MOBA_HEADS = 16
MOBA_KV_HEADS = 4
MOBA_GROUP = MOBA_HEADS // MOBA_KV_HEADS
MOBA_HEAD_DIM = 64
MOBA_BLOCK = 256
MOBA_TOPK = 3
MLA_HEADS = 16
MLA_Q_LORA = 256
MLA_KV_LORA = 128
MLA_NOPE = 64
MLA_ROPE = 32
MLA_V = 64
ROPE_THETA = 10000.0
N_GROUPS = 4
EXPERTS_PER_GROUP = 8
N_EXPERTS = N_GROUPS * EXPERTS_PER_GROUP
D_EXPERT = 256

LANES = 128
NEG = -1e30
LOG2E = 1.4426950408889634
VMEM_LIMIT = 56 * 1024 * 1024
MOBA_PAGES_PER_STEP = 32
PAGES_PER_STEP = 16

_NT = (((1,), (1,)), ((), ()))


def _nt_dot(a, b, **kw):
    return lax.dot_general(a, b, _NT, preferred_element_type=F32, **kw)


def _cparams(*sem):
    return pltpu.CompilerParams(dimension_semantics=sem, vmem_limit_bytes=VMEM_LIMIT)


def _rms_mod(x, g, shift, scale):
    y = x * lax.rsqrt(jnp.mean(x * x, axis=-1, keepdims=True) + NORM_EPS)
    return (y * g) * (1.0 + scale) + shift


def _top_mask_bias(vals, lanef, n_top, bias):
    for _ in range(n_top):
        m = jnp.max(vals, axis=-1, keepdims=True)
        idx = jnp.min(jnp.where(vals == m, lanef, 1e9), axis=-1, keepdims=True)
        idx = jnp.where(m > -jnp.inf, idx, -1.0)
        pick = lanef == idx
        bias = jnp.where(pick, 0.0, bias)
        vals = jnp.where(pick, -jnp.inf, vals)
    return bias


def _ada_kernel(c_ref, w_ref, b_ref, o_ref):
    c = c_ref[...]
    a = c * jax.nn.sigmoid(c)
    o_ref[0] = jnp.dot(a, w_ref[0], preferred_element_type=F32, precision=HIGHEST) + b_ref[0]


def _ada_call(c_all, w_ada, b_ada):
    depth, d, n = w_ada.shape
    rows = c_all.shape[0]
    tn = 512
    return pl.pallas_call(
        _ada_kernel,
        grid=(depth, n // tn),
        in_specs=[
            pl.BlockSpec((rows, d), lambda l, j: (0, 0)),
            pl.BlockSpec((1, d, tn), lambda l, j: (l, 0, j)),
            pl.BlockSpec((1, 1, tn), lambda l, j: (l, 0, j)),
        ],
        out_specs=pl.BlockSpec((1, rows, tn), lambda l, j: (l, 0, j)),
        out_shape=jax.ShapeDtypeStruct((depth, rows, n), F32),
        compiler_params=_cparams("arbitrary", "arbitrary"),
        name="ada_mod",
    )(c_all, w_ada, b_ada.reshape(depth, 1, n))


def _mod_spec(mod, tm):
    d = mod.shape[1]
    if mod.shape[0] == 1:
        return pl.BlockSpec((1, d), lambda i, *_: (0, 0))
    return pl.BlockSpec((tm, d), lambda i, *_: (i, 0))


def _norm_matmul_kernel(x_ref, g_ref, sh_ref, sc_ref, w_ref, o_ref):
    h = _rms_mod(x_ref[...], g_ref[...], sh_ref[...], sc_ref[...]).astype(BF16)
    o_ref[...] = jnp.dot(h, w_ref[...], preferred_element_type=F32)


def _norm_matmul(x, g, shift, scale, w, tm):
    m, d = x.shape
    tm = min(tm, m)
    n = w.shape[1]
    return pl.pallas_call(
        _norm_matmul_kernel,
        grid=(m // tm,),
        in_specs=[
            pl.BlockSpec((tm, d), lambda i: (i, 0)),
            pl.BlockSpec((1, d), lambda i: (0, 0)),
            _mod_spec(shift, tm),
            _mod_spec(scale, tm),
            pl.BlockSpec((d, n), lambda i: (0, 0)),
        ],
        out_specs=pl.BlockSpec((tm, n), lambda i: (i, 0)),
        out_shape=jax.ShapeDtypeStruct((m, n), F32),
        compiler_params=_cparams("arbitrary"),
        name="norm_matmul",
    )(x, g, shift, scale, w)


def _mm_res_kernel(a_ref, w_ref, x_ref, gate_ref, o_ref):
    y = jnp.dot(a_ref[...], w_ref[...], preferred_element_type=F32)
    o_ref[...] = x_ref[...] + gate_ref[...] * y


def _mm_res(a, w, x, gate, tm):
    m, k = a.shape
    tm = min(tm, m)
    n = w.shape[1]
    return pl.pallas_call(
        _mm_res_kernel,
        grid=(m // tm,),
        in_specs=[
            pl.BlockSpec((tm, k), lambda i: (i, 0)),
            pl.BlockSpec((k, n), lambda i: (0, 0)),
            pl.BlockSpec((tm, n), lambda i: (i, 0)),
            _mod_spec(gate, tm),
        ],
        out_specs=pl.BlockSpec((tm, n), lambda i: (i, 0)),
        out_shape=jax.ShapeDtypeStruct((m, n), F32),
        compiler_params=_cparams("arbitrary"),
        name="matmul_residual",
    )(a, w, x, gate)


def _bmm_kernel(a_ref, b_ref, o_ref):
    o_ref[0] = jnp.dot(a_ref[0], b_ref[0], preferred_element_type=F32, precision=HIGHEST)


def _bmm(a, b):
    n, m, k = a.shape
    p = b.shape[2]
    return pl.pallas_call(
        _bmm_kernel,
        grid=(n,),
        in_specs=[pl.BlockSpec((1, m, k), lambda i: (i, 0, 0)), pl.BlockSpec((1, k, p), lambda i: (i, 0, 0))],
        out_specs=pl.BlockSpec((1, m, p), lambda i: (i, 0, 0)),
        out_shape=jax.ShapeDtypeStruct((n, m, p), F32),
        compiler_params=_cparams("arbitrary"),
        name="fold_weights",
    )(a, b)


_QW = MOBA_HEADS * LANES
_KW = MOBA_KV_HEADS * LANES


def _moba_qkv_kernel(x_ref, g_ref, sh_ref, sc_ref, w_ref, q_ref, ka_ref, vp_ref, ko_ref, vo_ref, km_ref):
    i = pl.program_id(0)
    tm = x_ref.shape[0]
    half = LANES // 2

    @pl.when(i == 0)
    def _():
        km_ref[...] = jnp.zeros_like(km_ref)

    h = _rms_mod(x_ref[...], g_ref[...], sh_ref[...], sc_ref[...]).astype(BF16)
    y = jnp.dot(h, w_ref[...], preferred_element_type=F32)
    lane = lax.broadcasted_iota(jnp.int32, (tm, LANES), 1)
    lanef = lane.astype(F32)
    kvw = MOBA_KV_HEADS * MOBA_HEAD_DIM
    ko_ref[...] = y[:, _QW + 2 * _KW:_QW + 2 * _KW + kvw]
    vo_ref[...] = y[:, _QW + 2 * _KW + kvw:_QW + 2 * _KW + 2 * kvw]
    for k in range(MOBA_KV_HEADS):
        kp = y[:, _QW + k * LANES:_QW + (k + 1) * LANES]
        km_ref[k, pl.ds(half + i, 1), :] = jnp.mean(kp, axis=0, keepdims=True)
        ka_ref[:, k * LANES:(k + 1) * LANES] = jnp.where(lane == half + i, 1.0, kp).astype(BF16)
        vp = y[:, _QW + _KW + k * LANES:_QW + _KW + (k + 1) * LANES]
        vp_ref[:, k * LANES:(k + 1) * LANES] = jnp.where(lane == LANES - 1, 1.0, vp).astype(BF16)
    past = (lane >= half) & (lane < half + i)
    own_bias = jnp.where(lane == half + i, 0.0, NEG)
    for hd in range(MOBA_HEADS):
        qp = y[:, hd * LANES:(hd + 1) * LANES]
        gs = _nt_dot(qp, km_ref[hd // MOBA_GROUP], precision=HIGHEST)
        bias = _top_mask_bias(jnp.where(past, gs, -jnp.inf), lanef, MOBA_TOPK, own_bias)
        q_ref[hd] = jnp.where(lane < half, qp * (MOBA_HEAD_DIM ** -0.5 * LOG2E), bias).astype(BF16)


def _moba_qkv_call(x, g, shift, scale, w):
    t, d = x.shape
    tm = MOBA_BLOCK
    assert t // tm <= LANES // 2
    n = w.shape[1]
    kvw = MOBA_KV_HEADS * MOBA_HEAD_DIM
    return pl.pallas_call(
        _moba_qkv_kernel,
        grid=(t // tm,),
        in_specs=[
            pl.BlockSpec((tm, d), lambda i: (i, 0)),
            pl.BlockSpec((1, d), lambda i: (0, 0)),
            pl.BlockSpec((1, d), lambda i: (0, 0)),
            pl.BlockSpec((1, d), lambda i: (0, 0)),
            pl.BlockSpec((d, n), lambda i: (0, 0)),
        ],
        out_specs=[
            pl.BlockSpec((MOBA_HEADS, tm, LANES), lambda i: (0, i, 0)),
            pl.BlockSpec((tm, _KW), lambda i: (i, 0)),
            pl.BlockSpec((tm, _KW), lambda i: (i, 0)),
            pl.BlockSpec((tm, kvw), lambda i: (i, 0)),
            pl.BlockSpec((tm, kvw), lambda i: (i, 0)),
        ],
        out_shape=[
            jax.ShapeDtypeStruct((MOBA_HEADS, t, LANES), BF16),
            jax.ShapeDtypeStruct((t, _KW), BF16),
            jax.ShapeDtypeStruct((t, _KW), BF16),
            jax.ShapeDtypeStruct((t, kvw), F32),
            jax.ShapeDtypeStruct((t, kvw), F32),
        ],
        scratch_shapes=[pltpu.VMEM((MOBA_KV_HEADS, LANES, LANES), F32)],
        compiler_params=_cparams("arbitrary"),
        name="moba_qkv_gate",
    )(x, g, shift, scale, w)


def _moba_attn_kernel(q_ref, k_ref, v_ref, al_ref, sl_ref, o_ref, m_sc, acc_sc):
    i = pl.program_id(1)
    tq = q_ref.shape[1]
    blk = MOBA_BLOCK
    n_chains, rows, _ = m_sc.shape
    hpc = rows // tq
    m_sc[...] = jnp.full_like(m_sc, NEG)
    acc_sc[...] = jnp.zeros_like(acc_sc)

    def step(j, nblk, masked):
        tk = nblk * blk
        start = pl.multiple_of(j * blk, blk)
        dj = ((j - i) * blk).astype(F32)
        for u in range(n_chains):
            kv = u * hpc // MOBA_GROUP
            kb = k_ref[pl.ds(start, tk), kv * LANES:(kv + 1) * LANES]
            vb = v_ref[pl.ds(start, tk), kv * LANES:(kv + 1) * LANES]
            q = q_ref[u * hpc:(u + 1) * hpc].reshape(rows, LANES)
            s = _nt_dot(q, kb) + al_ref[u, :, :tk]
            if masked:
                row_pos = (nblk - 1) * blk + (lax.broadcasted_iota(jnp.int32, (rows, tk), 0) & (tq - 1))
                col = lax.broadcasted_iota(jnp.int32, (rows, tk), 1)
                s = jnp.where(col <= row_pos, s, NEG)
            c = sl_ref[u] * dj
            m_prev = m_sc[u]
            m_new = jnp.maximum(m_prev, jnp.max(s, axis=-1, keepdims=True) + c)
            shift = m_new - c
            p = jnp.exp2(s - jnp.concatenate([shift] * (tk // LANES), axis=1))
            acc_sc[u] = acc_sc[u] * jnp.exp2(m_prev - m_new) + jnp.dot(p.astype(BF16), vb, preferred_element_type=F32)
            m_sc[u] = m_new

    def past(jj, _):
        step(2 * jj, 2, False)
        return 0

    lax.fori_loop(0, i // 2, past, 0)

    @pl.when(i % 2 == 1)
    def _():
        step(i - 1, 2, True)

    @pl.when(i % 2 == 0)
    def _():
        step(i, 1, True)
    for u in range(n_chains):
        acc = acc_sc[u]
        o = acc / acc[:, LANES - 1:]
        for x in range(hpc):
            hd = u * hpc + x
            o_ref[:, hd * LANES:(hd + 1) * LANES] = o[x * tq:(x + 1) * tq].astype(BF16)


def _moba_attn_call(q_aug, k_aug, v_pad, al_tab, sl_tab):
    nh, t, _ = q_aug.shape
    tq = MOBA_BLOCK
    assert tq & (tq - 1) == 0
    kvps = 4
    n_steps = MOBA_KV_HEADS // kvps
    n_chains, rows = al_tab.shape[0] // n_steps, al_tab.shape[1]
    gw = kvps * MOBA_GROUP * LANES
    once = pl.Buffered(1)
    return pl.pallas_call(
        _moba_attn_kernel,
        grid=(n_steps, t // tq),
        in_specs=[
            pl.BlockSpec((kvps * MOBA_GROUP, tq, LANES), lambda k, i: (k, i, 0)),
            pl.BlockSpec((t, kvps * LANES), lambda k, i: (0, k), pipeline_mode=once),
            pl.BlockSpec((t, kvps * LANES), lambda k, i: (0, k), pipeline_mode=once),
            pl.BlockSpec((n_chains, rows, 2 * MOBA_BLOCK), lambda k, i: (k, 0, 0), pipeline_mode=once),
            pl.BlockSpec((n_chains, rows, LANES), lambda k, i: (k, 0, 0), pipeline_mode=once),
        ],
        out_specs=pl.BlockSpec((tq, gw), lambda k, i: (i, k)),
        out_shape=jax.ShapeDtypeStruct((t, nh * LANES), BF16),
        scratch_shapes=[pltpu.VMEM((n_chains, rows, LANES), F32), pltpu.VMEM((n_chains, rows, LANES), F32)],
        compiler_params=_cparams("arbitrary", "arbitrary"),
        name="moba_attn",
    )(q_aug, k_aug, v_pad, al_tab, sl_tab)


def _moe_kernel(x_ref, g_ref, sh_ref, sc_ref, gt_ref, gf_ref, wr_ref, wgu_ref, wdn_ref, o_ref,
                h_sc, gate_sc, acc_sc, *, final):
    e = pl.program_id(1)
    tm = x_ref.shape[0]
    lane = lax.broadcasted_iota(jnp.int32, (tm, LANES), 1)

    @pl.when(e == 0)
    def _():
        h = _rms_mod(x_ref[...], g_ref[...], sh_ref[...], sc_ref[...])
        h_sc[...] = h.astype(BF16)
        lanef = lane.astype(F32)
        logits = jnp.dot(h, wr_ref[...], preferred_element_type=F32, precision=HIGHEST)
        gl = jnp.where((lane >= N_EXPERTS) & (lane < N_EXPERTS + N_GROUPS), logits, -jnp.inf)
        gmax = jnp.max(gl, axis=-1, keepdims=True)
        g_sel = jnp.min(jnp.where(gl == gmax, lanef, 1e9), axis=-1, keepdims=True) - N_EXPERTS
        g_p = 1.0 / jnp.sum(jnp.exp(gl - gmax), axis=-1, keepdims=True)
        el = logits
        lo = g_sel * EXPERTS_PER_GROUP
        el = jnp.where((lanef >= lo) & (lanef < lo + EXPERTS_PER_GROUP), el, -jnp.inf)
        emax = jnp.max(el, axis=-1, keepdims=True)
        ex = jnp.exp(el - emax)
        prob = ex / jnp.sum(ex, axis=-1, keepdims=True)
        prob = jnp.where(el > -jnp.inf, prob, -jnp.inf)
        p1 = jnp.max(prob, axis=-1, keepdims=True)
        i1 = jnp.min(jnp.where(prob == p1, lanef, 1e9), axis=-1, keepdims=True)
        prob2 = jnp.where(lanef == i1, -jnp.inf, prob)
        p2 = jnp.max(prob2, axis=-1, keepdims=True)
        i2 = jnp.min(jnp.where(prob2 == p2, lanef, 1e9), axis=-1, keepdims=True)
        den = p1 + p2
        gate_sc[...] = jnp.where(lanef == i1, p1 / den * g_p, jnp.where(lanef == i2, p2 / den * g_p, 0.0))
        acc_sc[...] = jnp.zeros_like(acc_sc)

    gcol = jnp.sum(jnp.where(lane == e, gate_sc[...], 0.0), axis=-1, keepdims=True)
    gu = jnp.dot(h_sc[...], wgu_ref[0], preferred_element_type=F32)
    gg = gu[:, :D_EXPERT]
    a = (gg * jax.nn.sigmoid(gg)) * gu[:, D_EXPERT:] * gcol
    acc_sc[...] += jnp.dot(a.astype(BF16), wdn_ref[0], preferred_element_type=F32)

    @pl.when(e == pl.num_programs(1) - 1)
    def _():
        y = x_ref[...] + gt_ref[...] * acc_sc[...]
        if final:
            y = y * lax.rsqrt(jnp.mean(y * y, axis=-1, keepdims=True) + NORM_EPS) * gf_ref[...]
        o_ref[...] = y


def _moe_call(x, g, shift, scale, gate, g_final, w_route, w_gu, w_down, tm, final):
    m, d = x.shape
    tm = min(tm, m)
    ne, _, f2 = w_gu.shape
    fd = w_down.shape[1]
    return pl.pallas_call(
        functools.partial(_moe_kernel, final=final),
        grid=(m // tm, ne),
        in_specs=[
            pl.BlockSpec((tm, d), lambda i, e: (i, 0)),
            pl.BlockSpec((1, d), lambda i, e: (0, 0)),
            _mod_spec(shift, tm),
            _mod_spec(scale, tm),
            _mod_spec(gate, tm),
            pl.BlockSpec((1, d), lambda i, e: (0, 0)),
            pl.BlockSpec((d, LANES), lambda i, e: (0, 0)),
            pl.BlockSpec((1, d, f2), lambda i, e: (e, 0, 0)),
            pl.BlockSpec((1, fd, d), lambda i, e: (e, 0, 0)),
        ],
        out_specs=pl.BlockSpec((tm, d), lambda i, e: (i, 0)),
        out_shape=jax.ShapeDtypeStruct((m, d), F32),
        scratch_shapes=[pltpu.VMEM((tm, d), BF16), pltpu.VMEM((tm, LANES), F32), pltpu.VMEM((tm, d), F32)],
        compiler_params=_cparams("arbitrary", "arbitrary"),
        name="hier_moe",
    )(x, g, shift, scale, gate, g_final, w_route, w_gu, w_down)


_MLA_W1 = MLA_Q_LORA + MLA_KV_LORA + 2 * LANES


def _mla_qkv_kernel(x_ref, g_ref, sh_ref, sc_ref, cos_ref, sin_ref, w1_ref, gq_ref, gkv_ref, wl_ref, wr_ref, wrr_ref,
                    q_ref, ckv_ref, kr_ref, kf_ref):
    h = _rms_mod(x_ref[...], g_ref[...], sh_ref[...], sc_ref[...]).astype(BF16)
    a = jnp.dot(h, w1_ref[...], preferred_element_type=F32)
    cos = cos_ref[...]
    sin = sin_ref[...]
    cq = a[:, :MLA_Q_LORA]
    cqn = (cq * lax.rsqrt(jnp.mean(cq * cq, axis=-1, keepdims=True) + NORM_EPS) * gq_ref[...]).astype(BF16)
    kvc = a[:, MLA_Q_LORA:MLA_Q_LORA + MLA_KV_LORA]
    ckv = kvc * lax.rsqrt(jnp.mean(kvc * kvc, axis=-1, keepdims=True) + NORM_EPS) * gkv_ref[...]
    o = MLA_Q_LORA + MLA_KV_LORA
    kr = a[:, o:o + LANES] * cos + a[:, o + LANES:o + 2 * LANES] * sin
    ckv_ref[...] = ckv
    kr_ref[...] = kr[:, :MLA_ROPE]
    kf_ref[:, :LANES] = ckv.astype(BF16)
    lane = lax.broadcasted_iota(jnp.int32, kr.shape, 1)
    kf_ref[:, LANES:] = jnp.where(lane == LANES - 1, 1.0, kr).astype(BF16)
    scale = (MLA_NOPE + MLA_ROPE) ** -0.5 * LOG2E
    ql = jnp.dot(cqn, wl_ref[...], preferred_element_type=F32)
    qr = jnp.dot(cqn, wr_ref[...], preferred_element_type=F32)
    qrr = jnp.dot(cqn, wrr_ref[...], preferred_element_type=F32)
    for hd in range(MLA_HEADS):
        sl = slice(hd * LANES, (hd + 1) * LANES)
        q_ref[hd, :, :LANES] = (ql[:, sl] * scale).astype(BF16)
        q_ref[hd, :, LANES:] = ((qr[:, sl] * cos + qrr[:, sl] * sin) * scale).astype(BF16)


def _mla_qkv_call(x, g, shift, scale, cos, sin, w1, gq, gkv, wl, wr, wrr, tm):
    m, d = x.shape
    tm = min(tm, m)
    hw = MLA_HEADS * LANES
    full = lambda shape: pl.BlockSpec(shape, lambda i: (0,) * len(shape))
    return pl.pallas_call(
        _mla_qkv_kernel,
        grid=(m // tm,),
        in_specs=[
            pl.BlockSpec((tm, d), lambda i: (i, 0)),
            full((1, d)),
            _mod_spec(shift, tm),
            _mod_spec(scale, tm),
            pl.BlockSpec((tm, LANES), lambda i: (i, 0)),
            pl.BlockSpec((tm, LANES), lambda i: (i, 0)),
            full((d, _MLA_W1)),
            full((1, MLA_Q_LORA)),
            full((1, MLA_KV_LORA)),
            full((MLA_Q_LORA, hw)),
            full((MLA_Q_LORA, hw)),
            full((MLA_Q_LORA, hw)),
        ],
        out_specs=[
            pl.BlockSpec((MLA_HEADS, tm, 2 * LANES), lambda i: (0, i, 0)),
            pl.BlockSpec((tm, MLA_KV_LORA), lambda i: (i, 0)),
            pl.BlockSpec((tm, MLA_ROPE), lambda i: (i, 0)),
            pl.BlockSpec((tm, 2 * LANES), lambda i: (i, 0)),
        ],
        out_shape=[
            jax.ShapeDtypeStruct((MLA_HEADS, m, 2 * LANES), BF16),
            jax.ShapeDtypeStruct((m, MLA_KV_LORA), F32),
            jax.ShapeDtypeStruct((m, MLA_ROPE), F32),
            jax.ShapeDtypeStruct((m, 2 * LANES), BF16),
        ],
        compiler_params=_cparams("arbitrary"),
        name="mla_qkv",
    )(x, g, shift, scale, cos, sin, w1, gq, gkv, wl, wr, wrr)


def _mla_attn_kernel(q_ref, k_ref, o_ref, m_sc, acc_sc, *, tk, hpc, cpb):
    i = pl.program_id(0)
    nh, tq, w = q_ref.shape
    rows = hpc * tq
    n_chains = nh // hpc
    n_past = (i * tq) // tk
    m_sc[...] = jnp.full_like(m_sc, NEG)
    acc_sc[...] = jnp.zeros_like(acc_sc)

    def tile(j, masked):
        start = pl.multiple_of(j * tk, tk)
        kb = k_ref[pl.ds(start, tk), :]

        def chain_pair(cp, _):
            for u in range(cpb):
                c = cp * cpb + u
                q = q_ref[pl.ds(c * hpc, hpc)].reshape(rows, w)
                s = _nt_dot(q, kb)
                if masked:
                    row_pos = i * tq + (lax.broadcasted_iota(jnp.int32, (rows, tk), 0) & (tq - 1))
                    col_pos = start + lax.broadcasted_iota(jnp.int32, (rows, tk), 1)
                    s = jnp.where(col_pos <= row_pos, s, NEG)
                m_prev = m_sc[c]
                m_new = jnp.maximum(m_prev, jnp.max(s, axis=-1, keepdims=True))
                p = jnp.exp2(s - jnp.concatenate([m_new] * (tk // LANES), axis=1))
                alpha = jnp.exp2(m_prev - m_new)
                acc_sc[c] = (acc_sc[c] * jnp.concatenate([alpha] * (w // LANES), axis=1)
                             + jnp.dot(p.astype(BF16), kb, preferred_element_type=F32))
                m_sc[c] = m_new
            return 0

        if cpb == n_chains:
            chain_pair(0, 0)
        else:
            lax.fori_loop(0, n_chains // cpb, chain_pair, 0)

    def past_pair(jj, _):
        tile(2 * jj, False)
        tile(2 * jj + 1, False)
        return 0

    lax.fori_loop(0, n_past // 2, past_pair, 0)

    @pl.when(n_past % 2 == 1)
    def _():
        tile(n_past - 1, False)

    tile(n_past, True)
    for c in range(n_chains):
        acc = acc_sc[c]
        o = acc[:, :LANES] / acc[:, w - 1:]
        for x in range(hpc):
            hd = c * hpc + x
            o_ref[:, hd * LANES:(hd + 1) * LANES] = o[x * tq:(x + 1) * tq].astype(BF16)


def _mla_attn_call(q_full, k_full, tq, tk, hpc, cpb):
    nh, t, w = q_full.shape
    assert tq & (tq - 1) == 0 and tk % tq == 0 and t % tk == 0 and nh % (cpb * hpc) == 0
    n_chains, rows = nh // hpc, hpc * tq
    return pl.pallas_call(
        functools.partial(_mla_attn_kernel, tk=tk, hpc=hpc, cpb=cpb),
        grid=(t // tq,),
        in_specs=[
            pl.BlockSpec((nh, tq, w), lambda i: (0, i, 0)),
            pl.BlockSpec((t, w), lambda i: (0, 0)),
        ],
        out_specs=pl.BlockSpec((tq, nh * LANES), lambda i: (i, 0)),
        out_shape=jax.ShapeDtypeStruct((t, nh * LANES), BF16),
        scratch_shapes=[pltpu.VMEM((n_chains, rows, LANES), F32), pltpu.VMEM((n_chains, rows, w), F32)],
        compiler_params=_cparams("arbitrary"),
        name="mla_attn",
    )(q_full, k_full)


def _moba_sample_kernel(pt_ref, q_ref, slope_ref, tpos_ref, hk_ref, kn_ref, vn_ref, k_hbm, v_hbm, o_ref,
                        kbuf, vbuf, ksem, vsem, s_sc, p_sc, g_sc, b_sc, pn_sc, l_sc, *, n_pages, past_len):
    pps = MOBA_PAGES_PER_STEP
    seq = pl.program_id(0)
    n_seq = pl.num_programs(0)
    n_chunks = n_pages // pps
    n_blocks = n_pages // 2
    q = q_ref[0]
    rows = q.shape[0]
    lane = lax.broadcasted_iota(jnp.int32, (rows, LANES), 1)
    lanef = lane.astype(F32)

    def k_copies(sq, c):
        return [pltpu.make_async_copy(k_hbm.at[pt_ref[sq, c * pps + p]], kbuf.at[c, p], ksem.at[c])
                for p in range(pps)]

    def v_copies(sq, c):
        return [pltpu.make_async_copy(v_hbm.at[pt_ref[sq, c * pps + p]], vbuf.at[c, p], vsem.at[c])
                for p in range(pps)]

    def start(copies):
        for cp in copies:
            cp.start()

    def wait(copies):
        for cp in copies:
            cp.wait()

    @pl.when(seq == 0)
    def _():
        for c in range(n_chunks):
            start(k_copies(0, c))

    for c in range(n_chunks):
        start(v_copies(seq, c))

    for c in range(n_chunks):
        wait(k_copies(seq, c))
        for a in range(pps // 2):
            kb = jnp.concatenate([kbuf[c, 2 * a].astype(BF16), kbuf[c, 2 * a + 1].astype(BF16)], axis=1)
            s2 = jnp.dot(q, kb, preferred_element_type=F32)
            s_sc[c * pps + 2 * a] = s2[:, :LANES]
            s_sc[c * pps + 2 * a + 1] = s2[:, LANES:]

    @pl.when(seq + 1 < n_seq)
    def _():
        for c in range(n_chunks):
            start(k_copies(seq + 1, c))

    def softmax():
        for b in range(n_blocks):
            r = jnp.sum(s_sc[2 * b] + s_sc[2 * b + 1], axis=-1, keepdims=True)
            g_sc[b] = jnp.broadcast_to(r, (rows, LANES))
            b_sc[b] = jnp.full((rows, LANES), NEG, F32)
        for _ in range(min(MOBA_TOPK, n_blocks)):
            best = g_sc[0]
            for b in range(1, n_blocks):
                best = jnp.maximum(best, g_sc[b])
            idx = jnp.full((rows, LANES), float(n_blocks), F32)
            for b in reversed(range(n_blocks)):
                idx = jnp.where(g_sc[b] == best, float(b), idx)
            for b in range(n_blocks):
                hit = idx == float(b)
                b_sc[b] = jnp.where(hit, 0.0, b_sc[b])
                g_sc[b] = jnp.where(hit, -jnp.inf, g_sc[b])
        slope = jnp.broadcast_to(slope_ref[...], (rows, LANES))
        tpos = jnp.broadcast_to(tpos_ref[...], (rows, LANES))
        al_lane = slope * lanef
        sn = _nt_dot(q, kn_ref[0]) - slope * (tpos - (past_len + lanef))
        sn = jnp.where(past_len + lanef <= tpos, sn, NEG)
        mv = sn
        for b in range(n_blocks):
            for u in range(2):
                pg = 2 * b + u
                sc = s_sc[pg] + al_lane + (b_sc[b] + slope * (float(pg * LANES) - tpos))
                s_sc[pg] = sc
                mv = jnp.maximum(mv, sc)
        m = jnp.broadcast_to(jnp.max(mv, axis=-1, keepdims=True), (rows, LANES))
        pn = jnp.exp(sn - m)
        lv = pn
        for pg in range(n_pages):
            p = jnp.exp(s_sc[pg] - m)
            p_sc[pg] = p.astype(BF16)
            lv = lv + p
        l_sc[...] = jnp.sum(lv, axis=-1, keepdims=True)
        pn_sc[...] = pn.astype(BF16)

    softmax()
    acc = jnp.dot(pn_sc[...], vn_ref[0], preferred_element_type=F32)
    for c in range(n_chunks):
        wait(v_copies(seq, c))
        for a in range(pps // 2):
            pg = c * pps + 2 * a
            p2 = jnp.concatenate([p_sc[pg], p_sc[pg + 1]], axis=1)
            vt = jnp.concatenate([vbuf[c, 2 * a].astype(BF16), vbuf[c, 2 * a + 1].astype(BF16)], axis=1)
            acc += _nt_dot(p2, vt)
    o_ref[0] = (acc / l_sc[...] * hk_ref[...]).astype(BF16)


def _moba_sample_call(page_table, qbd, slope_col, tpos_col, hk_mask, k_new, v_new, cache_k, cache_v):
    nb, n_pages = page_table.shape
    rows, w = qbd.shape[1:]
    page = cache_k.shape[2]
    pps = MOBA_PAGES_PER_STEP
    assert page == LANES and n_pages % pps == 0 and pps % 2 == 0

    per_seq = lambda shape: pl.BlockSpec((1,) + shape, lambda b, pt: (b, 0, 0))
    const = lambda shape: pl.BlockSpec(shape, lambda b, pt: (0, 0))
    hbm = pl.BlockSpec(memory_space=pl.ANY)
    grid_spec = pltpu.PrefetchScalarGridSpec(
        num_scalar_prefetch=1,
        grid=(nb,),
        in_specs=[per_seq((rows, w)), const((rows, 1)), const((rows, 1)), const((rows, w)),
                  per_seq((LANES, w)), per_seq((LANES, w)), hbm, hbm],
        out_specs=per_seq((rows, w)),
        scratch_shapes=[
            pltpu.VMEM((n_pages // pps, pps, w, page), F32),
            pltpu.VMEM((n_pages // pps, pps, w, page), F32),
            pltpu.SemaphoreType.DMA((n_pages // pps,)),
            pltpu.SemaphoreType.DMA((n_pages // pps,)),
            pltpu.VMEM((n_pages, rows, LANES), F32),
            pltpu.VMEM((n_pages, rows, LANES), BF16),
            pltpu.VMEM((n_pages // 2, rows, LANES), F32),
            pltpu.VMEM((n_pages // 2, rows, LANES), F32),
            pltpu.VMEM((rows, LANES), BF16),
            pltpu.VMEM((rows, 1), F32),
        ],
    )
    return pl.pallas_call(
        functools.partial(_moba_sample_kernel, n_pages=n_pages, past_len=n_pages * page),
        grid_spec=grid_spec,
        out_shape=jax.ShapeDtypeStruct((nb, rows, w), BF16),
        compiler_params=_cparams("arbitrary"),
        name="moba_sample_attn",
    )(page_table, qbd, slope_col, tpos_col, hk_mask, k_new, v_new, cache_k, cache_v)


def _mla_sample_kernel(pt_ref, q_ref, trow_ref, kn_ref, c_hbm, r_hbm, o_ref, cbuf, rbuf, sem, cb_sc, s_sc, *, n_pages):
    cpp = PAGES_PER_STEP
    n_chunks = n_pages // cpp
    kw = cpp * LANES
    seq = pl.program_id(0)
    n_seq = pl.num_programs(0)
    slot = seq % 2
    q = q_ref[0]
    ql = q[:, :LANES]
    qr = q[:, LANES:LANES + MLA_ROPE]
    rows = q.shape[0]

    def seq_copies(sq, sl):
        out = []
        for p in range(n_pages):
            pid = pt_ref[sq, p]
            out.append(pltpu.make_async_copy(c_hbm.at[pid], cbuf.at[sl, p], sem.at[0, sl]))
            out.append(pltpu.make_async_copy(r_hbm.at[pid], rbuf.at[sl, p], sem.at[1, sl]))
        return out

    @pl.when(seq == 0)
    def _():
        for cp in seq_copies(0, 0):
            cp.start()

    for cp in seq_copies(seq, slot):
        cp.wait()

    @pl.when(seq + 1 < n_seq)
    def _():
        for cp in seq_copies(seq + 1, 1 - slot):
            cp.start()

    mv = jnp.full((rows, kw), NEG, F32)
    for c in range(n_chunks):
        cb = cbuf[slot, c * cpp:(c + 1) * cpp].reshape(kw, MLA_KV_LORA).astype(BF16)
        cb_sc[c] = cb
        rb = jnp.concatenate([rbuf[slot, c * cpp + p].astype(BF16) for p in range(cpp)], axis=1)
        s = _nt_dot(ql, cb) + jnp.dot(qr, rb, preferred_element_type=F32)
        s_sc[c] = s
        mv = jnp.maximum(mv, s)
    kn = kn_ref[0]
    lanef = lax.broadcasted_iota(jnp.int32, (rows, LANES), 1).astype(F32)
    sn = jnp.where(lanef <= trow_ref[...], _nt_dot(q, kn), NEG)
    m = jnp.maximum(jnp.max(mv, axis=-1, keepdims=True), jnp.max(sn, axis=-1, keepdims=True))
    pn = jnp.exp2(sn - m)
    acc = jnp.dot(pn.astype(BF16), kn[:, :LANES], preferred_element_type=F32)
    lv = jnp.zeros((rows, kw), F32)
    for c in range(n_chunks):
        pr = jnp.exp2(s_sc[c] - m)
        lv = lv + pr
        acc = acc + jnp.dot(pr.astype(BF16), cb_sc[c], preferred_element_type=F32)
    l = jnp.sum(lv, axis=-1, keepdims=True) + jnp.sum(pn, axis=-1, keepdims=True)
    o_ref[0] = (acc / l).astype(BF16)


def _mla_sample_call(page_table, q_rows, trow_col, k_new, cache_ckv, cache_krope):
    nb, n_pages = page_table.shape
    rows, w = q_rows.shape[1:]
    page = cache_ckv.shape[1]
    cpp = PAGES_PER_STEP
    assert page == LANES and n_pages % cpp == 0
    n_chunks = n_pages // cpp

    per_seq = lambda shape: pl.BlockSpec((1,) + shape, lambda b, pt: (b, 0, 0))
    hbm = pl.BlockSpec(memory_space=pl.ANY)
    grid_spec = pltpu.PrefetchScalarGridSpec(
        num_scalar_prefetch=1,
        grid=(nb,),
        in_specs=[per_seq((rows, w)), pl.BlockSpec((rows, 1), lambda b, pt: (0, 0)), per_seq((LANES, w)), hbm, hbm],
        out_specs=per_seq((rows, LANES)),
        scratch_shapes=[
            pltpu.VMEM((2, n_pages, page, MLA_KV_LORA), F32),
            pltpu.VMEM((2, n_pages, MLA_ROPE, page), F32),
            pltpu.SemaphoreType.DMA((2, 2)),
            pltpu.VMEM((n_chunks, cpp * page, MLA_KV_LORA), BF16),
            pltpu.VMEM((n_chunks, rows, cpp * page), F32),
        ],
    )
    return pl.pallas_call(
        functools.partial(_mla_sample_kernel, n_pages=n_pages),
        grid_spec=grid_spec,
        out_shape=jax.ShapeDtypeStruct((nb, rows, LANES), BF16),
        compiler_params=_cparams("arbitrary"),
        name="mla_sample_attn",
    )(page_table, q_rows, trow_col, k_new, cache_ckv, cache_krope)


def _pad_last(w, width):
    return jnp.pad(w, [(0, 0)] * (w.ndim - 1) + [(0, width - w.shape[-1])])


def _moba_weights(w_qkv, w_o):
    d = w_qkv.shape[0]
    nq = MOBA_HEADS * MOBA_HEAD_DIM
    nk = MOBA_KV_HEADS * MOBA_HEAD_DIM
    wq = _pad_last(w_qkv[:, :nq].reshape(d, MOBA_HEADS, MOBA_HEAD_DIM), LANES).reshape(d, _QW)
    wk = w_qkv[:, nq:nq + nk]
    wv = w_qkv[:, nq + nk:]
    wkp = _pad_last(wk.reshape(d, MOBA_KV_HEADS, MOBA_HEAD_DIM), LANES).reshape(d, _KW)
    wvp = _pad_last(wv.reshape(d, MOBA_KV_HEADS, MOBA_HEAD_DIM), LANES).reshape(d, _KW)
    w_fused = jnp.concatenate([wq, wkp, wvp, wk, wv], axis=1).astype(BF16)
    wo = w_o.reshape(MOBA_HEADS, MOBA_HEAD_DIM, d)
    wo_prompt = jnp.pad(wo, ((0, 0), (0, LANES - MOBA_HEAD_DIM), (0, 0))).reshape(_QW, d).astype(BF16)
    kv_of = jnp.arange(MOBA_HEADS) // MOBA_GROUP
    onehot = (kv_of[:, None] == jnp.arange(MOBA_KV_HEADS)[None, :]).astype(F32)
    wo_sample = (onehot[:, :, None, None] * wo[:, None, :, :]).reshape(MOBA_HEADS * nk, d).astype(BF16)
    return w_fused, wo_prompt, wo_sample


def _rotate_half_cols(w):
    half = w.shape[-1] // 2
    return jnp.concatenate([-w[..., half:], w[..., :half]], axis=-1)


def _mla_weights(w_dq, w_uq, w_dkv, w_uk, w_uv, w_o):
    d = w_dq.shape[0]
    w_r = w_dkv[:, MLA_KV_LORA:]
    w1 = jnp.concatenate([w_dq, w_dkv[:, :MLA_KV_LORA], _pad_last(w_r, LANES), _pad_last(_rotate_half_cols(w_r), LANES)],
                         axis=1).astype(BF16)
    wq = w_uq.reshape(MLA_Q_LORA, MLA_HEADS, MLA_NOPE + MLA_ROPE)
    wq_nope = wq[:, :, :MLA_NOPE].transpose(1, 0, 2)
    wuk_t = w_uk.transpose(1, 2, 0)
    w_lat = _bmm(wq_nope, wuk_t).transpose(1, 0, 2).reshape(MLA_Q_LORA, MLA_HEADS * LANES).astype(BF16)
    wq_rope = wq[:, :, MLA_NOPE:]
    w_qr = _pad_last(wq_rope, LANES).reshape(MLA_Q_LORA, MLA_HEADS * LANES).astype(BF16)
    w_qrr = _pad_last(_rotate_half_cols(wq_rope), LANES).reshape(MLA_Q_LORA, MLA_HEADS * LANES).astype(BF16)
    wuv = w_uv.transpose(1, 0, 2)
    wo = w_o.reshape(MLA_HEADS, MLA_V, d)
    w_vo = _bmm(wuv, wo).reshape(MLA_HEADS * MLA_KV_LORA, d).astype(BF16)
    return w1, w_lat, w_qr, w_qrr, w_vo


def _rope_tables(pos):
    inv = ROPE_THETA ** (-jnp.arange(0, MLA_ROPE, 2, dtype=F32) / MLA_ROPE)
    ang = pos.astype(F32)[:, None] * inv[None, :]
    cos = jnp.cos(ang)
    sin = jnp.sin(ang)
    return _pad_last(jnp.concatenate([cos, cos], axis=-1), LANES), _pad_last(jnp.concatenate([sin, sin], axis=-1), LANES)


def kernel(x_prompt, x_sample, cache_moba_k, cache_moba_v, cache_mla_ckv, cache_mla_krope, page_table, c_prompt, c_sample, w_ada, b_ada, g_norm1, g_norm2, moba_w_qkv, moba_w_o, mla_w_dq, mla_g_q, mla_w_uq, mla_w_dkv, mla_g_kv, mla_w_uk, mla_w_uv, mla_w_o, moe_w_group, moe_w_expert, moe_w_gu, moe_w_down, g_final):
    bp, seq, d = x_prompt.shape
    assert bp == 1
    nb, tdec, _ = x_sample.shape
    n_pool, page = cache_moba_k.shape[:2]
    n_pages = page_table.shape[1]
    past = n_pages * page
    ms = nb * tdec
    xp = x_prompt.reshape(seq, d)
    xs = x_sample.reshape(ms, d)

    c_all = jnp.concatenate([c_sample, c_prompt, jnp.zeros((7, d), F32)], axis=0)
    ada = _ada_call(c_all, w_ada, b_ada).reshape(w_ada.shape[0], c_all.shape[0], N_ADA, d)

    def mods(layer):
        mp = [ada[layer, nb:nb + 1, i] for i in range(N_ADA)]
        msm = [jnp.repeat(ada[layer, :nb, i], tdec, axis=0) for i in range(N_ADA)]
        return mp, msm

    row = lambda v: v.reshape(1, -1)
    w_route = [_pad_last(jnp.concatenate([moe_w_expert[l], moe_w_group[l]], axis=1), LANES) for l in range(2)]
    wgu = moe_w_gu.astype(BF16)
    wdn = moe_w_down.astype(BF16)
    gfin = row(g_final)

    def moe(x, layer, m, tm, final):
        return _moe_call(x, row(g_norm2[layer]), m[3], m[4], m[5], gfin, w_route[layer], wgu[layer], wdn[layer],
                         tm, final)

    mp, msm = mods(0)
    w_fused, wo_prompt, wo_sample = _moba_weights(moba_w_qkv, moba_w_o)
    g1 = row(g_norm1[0])
    slopes = jnp.exp2(-8.0 * jnp.arange(1, MOBA_HEADS + 1, dtype=F32) / MOBA_HEADS)

    q_aug, k_aug, v_pad, k_out, v_out = _moba_qkv_call(xp, g1, mp[0], mp[1], w_fused)
    hpc = 2
    slope_rows =jnp.repeat((slopes * LOG2E).reshape(MOBA_HEADS // hpc, hpc), MOBA_BLOCK, axis=1)
    al_tab = slope_rows[:, :, None] * jnp.arange(2 * MOBA_BLOCK, dtype=F32)
    sl_tab = jnp.broadcast_to(slope_rows[:, :, None], slope_rows.shape + (LANES,))
    o_pad = _moba_attn_call(q_aug, k_aug, v_pad, al_tab, sl_tab)
    xp = _mm_res(o_pad, wo_prompt, xp, mp[2], 512)
    xp = moe(xp, 0, mp, 1024, False)
    moba_k_prompt = k_out.reshape(1, seq, MOBA_KV_HEADS, MOBA_HEAD_DIM)
    moba_v_prompt = v_out.reshape(1, seq, MOBA_KV_HEADS, MOBA_HEAD_DIM)

    nq = MOBA_HEADS * MOBA_HEAD_DIM
    nk = MOBA_KV_HEADS * MOBA_HEAD_DIM
    qkv_s = _norm_matmul(xs, g1, msm[0], msm[1], moba_w_qkv.astype(BF16), 256)
    q_s = qkv_s[:, :nq].reshape(nb, tdec, MOBA_HEADS, 1, MOBA_HEAD_DIM) * MOBA_HEAD_DIM ** -0.5
    k_s = qkv_s[:, nq:nq + nk]
    v_s = qkv_s[:, nq + nk:]
    kv_of = jnp.arange(MOBA_HEADS) // MOBA_GROUP
    head_kv = (kv_of[:, None] == jnp.arange(MOBA_KV_HEADS)[None, :]).astype(F32)
    rows = tdec * MOBA_HEADS
    qbd = (q_s * head_kv[None, None, :, :, None]).reshape(nb, rows, nk).astype(BF16)
    slope_col = jnp.tile(slopes, tdec).reshape(rows, 1)
    tpos_col = jnp.repeat(past + jnp.arange(tdec, dtype=F32), MOBA_HEADS).reshape(rows, 1)
    hk_mask = jnp.tile(jnp.repeat(head_kv, MOBA_HEAD_DIM, axis=1), (tdec, 1))
    pad_new = lambda a: jnp.pad(a.reshape(nb, tdec, -1), ((0, 0), (0, LANES - tdec), (0, 0))).astype(BF16)
    om = _moba_sample_call(page_table, qbd, slope_col, tpos_col, hk_mask, pad_new(k_s), pad_new(v_s),
                           cache_moba_k.transpose(0, 2, 3, 1).reshape(n_pool, nk, page),
                           cache_moba_v.transpose(0, 2, 3, 1).reshape(n_pool, nk, page))
    xs = _mm_res(om.reshape(ms, MOBA_HEADS * nk), wo_sample, xs, msm[2], 256)
    xs = moe(xs, 0, msm, 512, False)
    moba_k_sample = k_s.reshape(nb, tdec, MOBA_KV_HEADS, MOBA_HEAD_DIM)
    moba_v_sample = v_s.reshape(nb, tdec, MOBA_KV_HEADS, MOBA_HEAD_DIM)

    mp, msm = mods(1)
    w1, w_lat, w_qr, w_qrr, w_vo = _mla_weights(mla_w_dq, mla_w_uq, mla_w_dkv, mla_w_uk, mla_w_uv, mla_w_o)
    g1 = row(g_norm1[1])
    gq = row(mla_g_q)
    gkv = row(mla_g_kv)

    cos_p, sin_p = _rope_tables(jnp.arange(seq))
    q_full, ckv_p, kr_p, k_full = _mla_qkv_call(xp, g1, mp[0], mp[1], cos_p, sin_p, w1, gq, gkv, w_lat, w_qr, w_qrr, 256)
    o_lat = _mla_attn_call(q_full, k_full, 256, 512, 2, 8)
    xp = _mm_res(o_lat, w_vo, xp, mp[2], 512)
    y_prompt = moe(xp, 1, mp, 1024, True).reshape(1, seq, d)

    cos_s, sin_s = _rope_tables(jnp.tile(past + jnp.arange(tdec), nb))
    q_fs, ckv_s, kr_s, k_fs = _mla_qkv_call(xs, g1, msm[0], msm[1], cos_s, sin_s, w1, gq, gkv, w_lat, w_qr, w_qrr, 256)
    q_rows = q_fs.reshape(MLA_HEADS, nb, tdec, 2 * LANES).transpose(1, 2, 0, 3).reshape(nb, tdec * MLA_HEADS, 2 * LANES)
    trow_col = jnp.repeat(jnp.arange(tdec, dtype=F32), MLA_HEADS).reshape(tdec * MLA_HEADS, 1)
    k_new = jnp.pad(k_fs.reshape(nb, tdec, 2 * LANES), ((0, 0), (0, LANES - tdec), (0, 0)))
    o_lat_s = _mla_sample_call(page_table, q_rows, trow_col, k_new, cache_mla_ckv, cache_mla_krope.transpose(0, 2, 1))
    xs = _mm_res(o_lat_s.reshape(ms, MLA_HEADS * LANES), w_vo, xs, msm[2], 256)
    y_sample = moe(xs, 1, msm, 512, True).reshape(nb, tdec, d)

    return (y_prompt, y_sample, moba_k_prompt, moba_v_prompt,
            ckv_p.reshape(1, seq, MLA_KV_LORA), kr_p.reshape(1, seq, MLA_ROPE),
            moba_k_sample, moba_v_sample,
            ckv_s.reshape(nb, tdec, MLA_KV_LORA), kr_s.reshape(nb, tdec, MLA_ROPE))
```
